```python
import jax, jax.numpy as jnp
from jax import lax
import numpy as np

D_MODEL = 1024
BATCH = 16
SEQ = 2048
DEPTH = 2

PLE_DIM = 256
CHUNK = 128
A_HEADS = 4
A_HEAD_DIM = 128
A_WIDTH = A_HEADS * A_HEAD_DIM
B_HEADS = 8
B_HEAD_DIM = 64
B_WIDTH = B_HEADS * B_HEAD_DIM
MIX_WIDTH = A_WIDTH + B_WIDTH
IN_WIDTH = 2 * A_WIDTH + 3 * B_WIDTH
B_CONV = 3
C_WIDTH = D_MODEL
C_CONV = 31
D_FF = 2816
FFN_CONV = 3
N_EVEN = (DEPTH + 1) // 2
N_ODD = DEPTH // 2
EPS = 1e-6

kernel_name = "hybrid_gmlp_shortconv_conformer_block"


def rmsnorm(x, g):
    xf = x.astype(jnp.float32)
    y = xf * lax.rsqrt(jnp.mean(xf * xf, axis=-1, keepdims=True) + EPS)
    return (y * g.astype(jnp.float32)).astype(x.dtype)


def layernorm(x, g, b):
    xf = x.astype(jnp.float32)
    mu = jnp.mean(xf, axis=-1, keepdims=True)
    xc = xf - mu
    var = jnp.mean(xc * xc, axis=-1, keepdims=True)
    y = xc * lax.rsqrt(var + EPS) * g.astype(jnp.float32) + b.astype(jnp.float32)
    return y.astype(x.dtype)


def causal_dwconv(x, w):
    K, C = w.shape
    return lax.conv_general_dilated(
        x, w[:, None, :].astype(x.dtype), window_strides=(1,), padding=[(K - 1, 0)],
        dimension_numbers=('NWC', 'WIO', 'NWC'), feature_group_count=C)


def gmlp_spatial_gate(u, v, w_s, b_s, ln_g, ln_b):
    bsz, s, _ = u.shape
    n = s // CHUNK
    vh = v.reshape(bsz, n, CHUNK, A_HEADS, A_HEAD_DIM)
    vh = layernorm(vh, ln_g.reshape(A_HEADS, A_HEAD_DIM), ln_b.reshape(A_HEADS, A_HEAD_DIM))
    mask = jnp.tril(jnp.ones((CHUNK, CHUNK), dtype=bool))
    w = jnp.where(mask[None], w_s, jnp.zeros_like(w_s)).astype(v.dtype)
    mixed = jnp.einsum('hts,bnshd->bnthd', w, vh) + b_s.T.astype(v.dtype)[None, None, :, :, None]
    return u * mixed.reshape(bsz, s, A_WIDTH)


def even_mixer(h, w_in, a_ws, a_bs, a_ln_g, a_ln_b, b_conv_w, w_out):
    z = h @ w_in
    a_u, a_v, b_b, b_c, b_h = jnp.split(
        z, [A_WIDTH, 2 * A_WIDTH, 2 * A_WIDTH + B_WIDTH, 2 * A_WIDTH + 2 * B_WIDTH], axis=-1)
    a_out = gmlp_spatial_gate(jax.nn.gelu(a_u), jax.nn.gelu(a_v), a_ws, a_bs, a_ln_g, a_ln_b)
    b_out = b_b * causal_dwconv(b_c * b_h, b_conv_w)
    return jnp.concatenate([a_out, b_out], axis=-1) @ w_out


def conformer_conv(h, w_in, b_in, dw_w, dw_b, ln_g, ln_b, w_out, b_out):
    z = h @ w_in + b_in
    a, g = jnp.split(z, 2, axis=-1)
    y = a * jax.nn.sigmoid(g)
    y = causal_dwconv(y, dw_w) + dw_b
    y = jax.nn.silu(layernorm(y, ln_g, ln_b))
    return y @ w_out + b_out


def conv_ffn(h, w_up, dw_w, dw_b, w_down):
    z = causal_dwconv(h @ w_up, dw_w) + dw_b
    g, u = jnp.split(z, 2, axis=-1)
    return (jax.nn.silu(g) * u) @ w_down


def setup_inputs(seed: int = 0) -> dict:
    key = jax.random.key(seed)
    ks = iter(jax.random.split(key, 64))

    def nrm(shape, scale):
        return jax.random.normal(next(ks), shape, jnp.float32) * scale

    def gain(shape):
        return 1.0 + nrm(shape, 0.02)

    d = D_MODEL
    return {
        "x": nrm((BATCH, SEQ, d), 1.0),
        "p": nrm((DEPTH, BATCH, SEQ, PLE_DIM), 1.0),
        "ev_norm": gain((N_EVEN, d)),
        "ev_w_in": nrm((N_EVEN, d, IN_WIDTH), d ** -0.5),
        "ev_a_ws": nrm((N_EVEN, A_HEADS, CHUNK, CHUNK), CHUNK ** -0.5),
        "ev_a_bs": gain((N_EVEN, A_HEADS, CHUNK)),
        "ev_a_ln_g": gain((N_EVEN, A_WIDTH)),
        "ev_a_ln_b": nrm((N_EVEN, A_WIDTH), 0.02),
        "ev_b_conv_w": nrm((N_EVEN, B_CONV, B_WIDTH), B_CONV ** -0.5),
        "ev_w_out": nrm((N_EVEN, MIX_WIDTH, d), MIX_WIDTH ** -0.5),
        "od_norm": gain((N_ODD, d)),
        "od_w_in": nrm((N_ODD, d, 2 * C_WIDTH), d ** -0.5),
        "od_b_in": nrm((N_ODD, 2 * C_WIDTH), 0.02),
        "od_dw_w": nrm((N_ODD, C_CONV, C_WIDTH), C_CONV ** -0.5),
        "od_dw_b": nrm((N_ODD, C_WIDTH), 0.02),
        "od_ln_g": gain((N_ODD, C_WIDTH)),
        "od_ln_b": nrm((N_ODD, C_WIDTH), 0.02),
        "od_w_out": nrm((N_ODD, C_WIDTH, d), C_WIDTH ** -0.5),
        "od_b_out": nrm((N_ODD, d), 0.02),
        "ffn_norm": gain((DEPTH, d)),
        "ffn_w_up": nrm((DEPTH, d, 2 * D_FF), d ** -0.5),
        "ffn_dw_w": nrm((DEPTH, FFN_CONV, 2 * D_FF), FFN_CONV ** -0.5),
        "ffn_dw_b": nrm((DEPTH, 2 * D_FF), 0.02),
        "ffn_w_down": nrm((DEPTH, D_FF, d), D_FF ** -0.5),
        "ple_w_p": nrm((DEPTH, PLE_DIM, d), PLE_DIM ** -0.5),
        "ple_norm": gain((DEPTH, d)),
        "ple_w_g": nrm((DEPTH, d, d), d ** -0.5),
        "final_norm": gain((d,)),
    }


def reference(x, p, ev_norm, ev_w_in, ev_a_ws, ev_a_bs, ev_a_ln_g, ev_a_ln_b, ev_b_conv_w, ev_w_out,
              od_norm, od_w_in, od_b_in, od_dw_w, od_dw_b, od_ln_g, od_ln_b, od_w_out, od_b_out,
              ffn_norm, ffn_w_up, ffn_dw_w, ffn_dw_b, ffn_w_down, ple_w_p, ple_norm, ple_w_g,
              final_norm):
    r = x
    for i in range(DEPTH):
        j = i // 2
        if i % 2 == 0:
            h = rmsnorm(r, ev_norm[j])
            r = r + even_mixer(h, ev_w_in[j], ev_a_ws[j], ev_a_bs[j], ev_a_ln_g[j], ev_a_ln_b[j],
                               ev_b_conv_w[j], ev_w_out[j])
        else:
            h = rmsnorm(r, od_norm[j])
            r = r + conformer_conv(h, od_w_in[j], od_b_in[j], od_dw_w[j], od_dw_b[j],
                                   od_ln_g[j], od_ln_b[j], od_w_out[j], od_b_out[j])
        h = rmsnorm(r, ffn_norm[i])
        r = r + conv_ffn(h, ffn_w_up[i], ffn_dw_w[i], ffn_dw_b[i], ffn_w_down[i])
        gate = jax.nn.sigmoid(rmsnorm(r, ple_norm[i]) @ ple_w_g[i])
        r = r + gate * (p[i] @ ple_w_p[i])
    return rmsnorm(r, final_norm)
```

```python
import functools

import jax
import jax.numpy as jnp
from jax import lax
from jax.experimental import pallas as pl
from jax.experimental.pallas import tpu as pltpu

D_MODEL = 1024
PLE_DIM = 256
CHUNK = 128
A_HEADS = 4
A_HEAD_DIM = 128
A_WIDTH = A_HEADS * A_HEAD_DIM
B_WIDTH = 512
MIX_WIDTH = A_WIDTH + B_WIDTH
IN_WIDTH = 2 * A_WIDTH + 3 * B_WIDTH
B_CONV = 3
C_WIDTH = D_MODEL
C_CONV = 31
D_FF = 2816
FFN_CONV = 3
EPS = 1e-6

SUBLANES = 8
SEQ_TILE = 256
FF_CHUNK = 256
N_FF_CHUNKS = D_FF // FF_CHUNK
C_HALO = 32
C_ROW_BLOCK = 64
C_LANE_BLOCK = 512
VMEM_LIMIT_BYTES = 56 * 1024 * 1024

assert D_FF % FF_CHUNK == 0 and SEQ_TILE % CHUNK == 0 and C_HALO >= C_CONV - 1

_bf16 = jnp.bfloat16
_f32 = jnp.float32


def _dot(a, b):
    return jnp.dot(a, b, preferred_element_type=_f32)


def _rms(x, g):
    ms = jnp.mean(x * x, axis=-1, keepdims=True)
    return x * lax.rsqrt(ms + EPS) * g


def _layernorm(x, g, b):
    mu = jnp.mean(x, axis=-1, keepdims=True)
    xc = x - mu
    var = jnp.mean(xc * xc, axis=-1, keepdims=True)
    return xc * lax.rsqrt(var + EPS) * g + b


def _conv3(x, buf_ref, carry, w_ref, col0, ncols):
    t = x.shape[0]
    buf_ref[0:SUBLANES, :] = carry
    buf_ref[SUBLANES:SUBLANES + t, :] = x
    w = w_ref[:, col0:col0 + ncols]
    x1 = buf_ref[SUBLANES - 1:SUBLANES - 1 + t, :]
    x2 = buf_ref[SUBLANES - 2:SUBLANES - 2 + t, :]
    return x * w[2:3, :] + x1 * w[1:2, :] + x2 * w[0:1, :]


def _even_kernel(r_ref, nrm_ref, win_ref, ws_ref, bs_ref, lng_ref, lnb_ref, cw_ref, wout_ref,
                 o_ref, carry_ref, buf_ref):
    ts = r_ref.shape[0]
    n_chunks = ts // CHUNK

    @pl.when(pl.program_id(1) == 0)
    def _():
        carry_ref[...] = jnp.zeros_like(carry_ref)

    r = r_ref[...]
    h = _rms(r, nrm_ref[...]).astype(_bf16)
    z = _dot(h, win_ref[...])
    a_u = jax.nn.gelu(z[:, 0:A_WIDTH])
    a_v = jax.nn.gelu(z[:, A_WIDTH:2 * A_WIDTH])
    b_b = z[:, 2 * A_WIDTH:2 * A_WIDTH + B_WIDTH]
    b_c = z[:, 2 * A_WIDTH + B_WIDTH:2 * A_WIDTH + 2 * B_WIDTH]
    b_h = z[:, 2 * A_WIDTH + 2 * B_WIDTH:IN_WIDTH]

    row = lax.broadcasted_iota(jnp.int32, (CHUNK, CHUNK), 0)
    col = lax.broadcasted_iota(jnp.int32, (CHUNK, CHUNK), 1)
    tril = row >= col
    a_parts = []
    for hd in range(A_HEADS):
        lo = hd * A_HEAD_DIM
        vh = _layernorm(a_v[:, lo:lo + A_HEAD_DIM], lng_ref[:, lo:lo + A_HEAD_DIM],
                        lnb_ref[:, lo:lo + A_HEAD_DIM]).astype(_bf16)
        rhs = jnp.concatenate([vh[n * CHUNK:(n + 1) * CHUNK, :] for n in range(n_chunks)], axis=1)
        w = jnp.where(tril, ws_ref[hd], 0.0).astype(_bf16)
        mixed = _dot(w, rhs) + bs_ref[:, hd:hd + 1]
        mixed = jnp.concatenate(
            [mixed[:, n * A_HEAD_DIM:(n + 1) * A_HEAD_DIM] for n in range(n_chunks)], axis=0)
        a_parts.append(a_u[:, lo:lo + A_HEAD_DIM] * mixed)

    x = b_c * b_h
    b_out = b_b * _conv3(x, buf_ref, carry_ref[...], cw_ref, 0, B_WIDTH)
    carry_ref[...] = x[ts - SUBLANES:ts, :]

    mix = jnp.concatenate(a_parts + [b_out], axis=1).astype(_bf16)
    o_ref[...] = r + _dot(mix, wout_ref[...])


def _odd_kernel(r_ref, nrm_ref, win_ref, bin_ref, dww_ref, dwb_ref, lng_ref, lnb_ref, wout_ref,
                bout_ref, o_ref, ybuf_ref, conv_ref):
    ts = r_ref.shape[0]

    @pl.when(pl.program_id(1) == 0)
    def _():
        ybuf_ref[0:C_HALO, :] = jnp.zeros((C_HALO, C_WIDTH), _f32)

    r = r_ref[...]
    h = _rms(r, nrm_ref[...]).astype(_bf16)
    z = _dot(h, win_ref[...]) + bin_ref[...]
    ybuf_ref[C_HALO:C_HALO + ts, :] = z[:, 0:C_WIDTH] * jax.nn.sigmoid(z[:, C_WIDTH:2 * C_WIDTH])

    base = C_HALO - (C_CONV - 1)
    n_lane_blocks = C_WIDTH // C_LANE_BLOCK

    for rb in range(ts // C_ROW_BLOCK):
        r0 = rb * C_ROW_BLOCK
        for lb in range(n_lane_blocks):
            c0 = lb * C_LANE_BLOCK
            acc = jnp.broadcast_to(dwb_ref[:, c0:c0 + C_LANE_BLOCK], (C_ROW_BLOCK, C_LANE_BLOCK))
            for k in range(C_CONV):
                acc = acc + (ybuf_ref[r0 + base + k:r0 + base + k + C_ROW_BLOCK, c0:c0 + C_LANE_BLOCK]
                             * dww_ref[k:k + 1, c0:c0 + C_LANE_BLOCK])
            conv_ref[r0:r0 + C_ROW_BLOCK, c0:c0 + C_LANE_BLOCK] = acc
    ybuf_ref[0:C_HALO, :] = ybuf_ref[ts:ts + C_HALO, :]

    y = _layernorm(conv_ref[...], lng_ref[...], lnb_ref[...])
    y = (y * jax.nn.sigmoid(y)).astype(_bf16)
    o_ref[...] = r + _dot(y, wout_ref[...]) + bout_ref[...]


def _ffn_kernel(r_ref, p_ref, nrm_ref, wup_ref, dww_ref, dwb_ref, wdn_ref, plen_ref, wg_ref, wp_ref,
                fin_ref, o_ref, carry_ref, buf_ref, *, final):
    ts = r_ref.shape[0]

    @pl.when(pl.program_id(1) == 0)
    def _():
        carry_ref[...] = jnp.zeros_like(carry_ref)

    r = r_ref[...]
    h = _rms(r, nrm_ref[...]).astype(_bf16)
    acc = jnp.zeros((ts, D_MODEL), _f32)
    for c in range(N_FF_CHUNKS):
        act = None
        for half in range(2):
            col0 = half * D_FF + c * FF_CHUNK
            zc = _dot(h, wup_ref[:, col0:col0 + FF_CHUNK])
            conv = _conv3(zc, buf_ref.at[half], carry_ref[:, col0:col0 + FF_CHUNK], dww_ref, col0,
                          FF_CHUNK) + dwb_ref[:, col0:col0 + FF_CHUNK]
            carry_ref[:, col0:col0 + FF_CHUNK] = zc[ts - SUBLANES:ts, :]
            act = conv * jax.nn.sigmoid(conv) if half == 0 else act * conv
        acc = acc + _dot(act.astype(_bf16), wdn_ref[c * FF_CHUNK:(c + 1) * FF_CHUNK, :])
    r2 = r + acc

    gate = jax.nn.sigmoid(_dot(_rms(r2, plen_ref[...]).astype(_bf16), wg_ref[...]))
    r3 = r2 + gate * _dot(p_ref[...].astype(_bf16), wp_ref[...])
    if final:
        r3 = _rms(r3, fin_ref[...])
    o_ref[...] = r3


def _rows(n):
    return pl.BlockSpec((None, SEQ_TILE, n), lambda b, j: (b, j, 0))


def _whole(a):
    nd = a.ndim
    return pl.BlockSpec(a.shape, lambda b, j: (0,) * nd, pipeline_mode=pl.Buffered(1))


def _call(body, name, batch, seq, row_inputs, params, scratch):
    grid = (batch, seq // SEQ_TILE)
    return pl.pallas_call(
        body,
        name=name,
        grid=grid,
        in_specs=[_rows(a.shape[-1]) for a in row_inputs] + [_whole(a) for a in params],
        out_specs=_rows(D_MODEL),
        out_shape=jax.ShapeDtypeStruct((batch, seq, D_MODEL), _f32),
        scratch_shapes=scratch,
        compiler_params=pltpu.CompilerParams(
            dimension_semantics=("arbitrary", "arbitrary"),
            vmem_limit_bytes=VMEM_LIMIT_BYTES),
    )(*row_inputs, *params)


def _row(v):
    return v.reshape(1, -1).astype(_f32)


def _even_layer(r, norm, w_in, a_ws, a_bs, a_ln_g, a_ln_b, b_conv_w, w_out):
    batch, seq, _ = r.shape
    params = [_row(norm), w_in.astype(_bf16), a_ws, a_bs.T, _row(a_ln_g), _row(a_ln_b), b_conv_w,
              w_out.astype(_bf16)]
    scratch = [pltpu.VMEM((SUBLANES, B_WIDTH), _f32),
               pltpu.VMEM((SUBLANES + SEQ_TILE, B_WIDTH), _f32)]
    return _call(_even_kernel, "even_mixer", batch, seq, [r], params, scratch)


def _odd_layer(r, norm, w_in, b_in, dw_w, dw_b, ln_g, ln_b, w_out, b_out):
    batch, seq, _ = r.shape
    params = [_row(norm), w_in.astype(_bf16), _row(b_in), dw_w, _row(dw_b), _row(ln_g), _row(ln_b),
              w_out.astype(_bf16), _row(b_out)]
    scratch = [pltpu.VMEM((C_HALO + SEQ_TILE, C_WIDTH), _f32),
               pltpu.VMEM((SEQ_TILE, C_WIDTH), _f32)]
    return _call(_odd_kernel, "odd_mixer", batch, seq, [r], params, scratch)


def _ffn_layer(r, p, norm, w_up, dw_w, dw_b, w_down, ple_norm, w_g, w_p, final_norm, final):
    batch, seq, _ = r.shape
    params = [_row(norm), w_up.astype(_bf16), dw_w, _row(dw_b), w_down.astype(_bf16), _row(ple_norm),
              w_g.astype(_bf16), w_p.astype(_bf16), _row(final_norm)]
    scratch = [pltpu.VMEM((SUBLANES, 2 * D_FF), _f32),
               pltpu.VMEM((2, SUBLANES + SEQ_TILE, FF_CHUNK), _f32)]
    return _call(functools.partial(_ffn_kernel, final=final), "ffn_final" if final else "ffn",
                 batch, seq, [r, p], params, scratch)


def kernel(x, p, ev_norm, ev_w_in, ev_a_ws, ev_a_bs, ev_a_ln_g, ev_a_ln_b, ev_b_conv_w, ev_w_out,
           od_norm, od_w_in, od_b_in, od_dw_w, od_dw_b, od_ln_g, od_ln_b, od_w_out, od_b_out,
           ffn_norm, ffn_w_up, ffn_dw_w, ffn_dw_b, ffn_w_down, ple_w_p, ple_norm, ple_w_g,
           final_norm):
    depth = p.shape[0]
    r = x
    for i in range(depth):
        j = i // 2
        if i % 2 == 0:
            r = _even_layer(r, ev_norm[j], ev_w_in[j], ev_a_ws[j], ev_a_bs[j], ev_a_ln_g[j],
                            ev_a_ln_b[j], ev_b_conv_w[j], ev_w_out[j])
        else:
            r = _odd_layer(r, od_norm[j], od_w_in[j], od_b_in[j], od_dw_w[j], od_dw_b[j],
                           od_ln_g[j], od_ln_b[j], od_w_out[j], od_b_out[j])
        r = _ffn_layer(r, p[i], ffn_norm[i], ffn_w_up[i], ffn_dw_w[i], ffn_dw_b[i], ffn_w_down[i],
                       ple_norm[i], ple_w_g[i], ple_w_p[i], final_norm, final=(i == depth - 1))
    return r
```

```python
import functools

import jax
import jax.numpy as jnp
from jax import lax
from jax.experimental import pallas as pl
from jax.experimental.pallas import tpu as pltpu

D_MODEL = 1024
PLE_DIM = 256
CHUNK = 128
A_HEADS = 4
A_HEAD_DIM = 128
A_WIDTH = A_HEADS * A_HEAD_DIM
B_WIDTH = 512
MIX_WIDTH = A_WIDTH + B_WIDTH
IN_WIDTH = 2 * A_WIDTH + 3 * B_WIDTH
B_CONV = 3
C_WIDTH = D_MODEL
C_CONV = 31
D_FF = 2816
FFN_CONV = 3
EPS = 1e-6

SUBLANES = 8
LANES = 128
ROW_PHASES = 2
SEQ_TILE = 256
FF_CHUNK = 256
N_FF_CHUNKS = D_FF // FF_CHUNK
FF_SLOTS = 4
C_HALO = 32
C_PHASES = 4
C_ROWS = 16
VMEM_LIMIT_BYTES = 56 * 1024 * 1024

assert D_FF % FF_CHUNK == 0 and SEQ_TILE % CHUNK == 0 and C_HALO >= C_CONV - 1

_bf16 = jnp.bfloat16
_f32 = jnp.float32


def _dot(a, b):
    return jnp.dot(a, b, preferred_element_type=_f32)


def _rms(x, g):
    ms = jnp.mean(x * x, axis=-1, keepdims=True)
    return x * lax.rsqrt(ms + EPS) * g


def _layernorm(x, g, b):
    mu = jnp.mean(x, axis=-1, keepdims=True)
    xc = x - mu
    var = jnp.mean(xc * xc, axis=-1, keepdims=True)
    return xc * lax.rsqrt(var + EPS) * g + b


def _conv3_phased(slab_ref, t, w):
    n = t // ROW_PHASES
    base = SUBLANES - (FFN_CONV - 1)
    taps = [slab_ref[pl.ds(base + i, n, stride=ROW_PHASES), :]
            for i in range(ROW_PHASES + FFN_CONV - 1)]
    return jnp.concatenate(
        [sum(taps[ph + k] * w[k:k + 1, :] for k in range(FFN_CONV)) for ph in range(ROW_PHASES)],
        axis=0)


def _natural_rows(x, slabs_ref, phases=ROW_PHASES):
    t = x.shape[0]
    n = t // phases
    n_slabs = x.shape[1] // LANES
    for s in range(n_slabs):
        for ph in range(phases):
            slabs_ref[s, pl.ds(ph, n, stride=phases), :] = (
                x[ph * n:(ph + 1) * n, s * LANES:(s + 1) * LANES])
    return jnp.concatenate([slabs_ref[s] for s in range(n_slabs)], axis=1)


def _even_kernel(r_ref, nrm_ref, win_ref, ws_ref, bs_ref, lng_ref, lnb_ref, cw_ref, wout_ref,
                 o_ref, buf_ref, unperm_ref):
    ts = r_ref.shape[0]
    n_chunks = ts // CHUNK

    @pl.when(pl.program_id(1) == 0)
    def _():
        buf_ref[:, 0:SUBLANES, :] = jnp.zeros((B_WIDTH // LANES, SUBLANES, LANES), _f32)

    r = r_ref[...]
    h = _rms(r, nrm_ref[...]).astype(_bf16)
    z = _dot(h, win_ref[...])
    a_u = jax.nn.gelu(z[:, 0:A_WIDTH])
    a_v = jax.nn.gelu(z[:, A_WIDTH:2 * A_WIDTH])
    b_b = z[:, 2 * A_WIDTH:2 * A_WIDTH + B_WIDTH]
    b_c = z[:, 2 * A_WIDTH + B_WIDTH:2 * A_WIDTH + 2 * B_WIDTH]
    b_h = z[:, 2 * A_WIDTH + 2 * B_WIDTH:IN_WIDTH]

    row = lax.broadcasted_iota(jnp.int32, (CHUNK, CHUNK), 0)
    col = lax.broadcasted_iota(jnp.int32, (CHUNK, CHUNK), 1)
    tril = row >= col
    a_parts = []
    for hd in range(A_HEADS):
        lo = hd * A_HEAD_DIM
        vh = _layernorm(a_v[:, lo:lo + A_HEAD_DIM], lng_ref[:, lo:lo + A_HEAD_DIM],
                        lnb_ref[:, lo:lo + A_HEAD_DIM]).astype(_bf16)
        rhs = jnp.concatenate([vh[n * CHUNK:(n + 1) * CHUNK, :] for n in range(n_chunks)], axis=1)
        w = jnp.where(tril, ws_ref[hd], 0.0).astype(_bf16)
        mixed = _dot(w, rhs) + bs_ref[:, hd:hd + 1]
        mixed = jnp.concatenate(
            [mixed[:, n * A_HEAD_DIM:(n + 1) * A_HEAD_DIM] for n in range(n_chunks)], axis=0)
        a_parts.append(a_u[:, lo:lo + A_HEAD_DIM] * mixed)

    x = b_c * b_h
    convs = []
    for s in range(B_WIDTH // LANES):
        slab = buf_ref.at[s]
        slab[SUBLANES:SUBLANES + ts, :] = x[:, s * LANES:(s + 1) * LANES]
        convs.append(_conv3_phased(slab, ts, cw_ref[:, s * LANES:(s + 1) * LANES]))
        slab[0:SUBLANES, :] = slab[ts:ts + SUBLANES, :]
    b_out = b_b * _natural_rows(jnp.concatenate(convs, axis=1), unperm_ref)

    mix = jnp.concatenate(a_parts + [b_out], axis=1).astype(_bf16)
    o_ref[...] = r + _dot(mix, wout_ref[...])


def _odd_kernel(r_ref, nrm_ref, win_ref, bin_ref, dww_ref, dwb_ref, lng_ref, lnb_ref, wout_ref,
                bout_ref, o_ref, ybuf_ref, conv_ref, unperm_ref):
    ts = r_ref.shape[0]
    n_slabs = C_WIDTH // LANES

    @pl.when(pl.program_id(1) == 0)
    def _():
        ybuf_ref[:, 0:C_HALO, :] = jnp.zeros((n_slabs, C_HALO, LANES), _f32)

    r = r_ref[...]
    h = _rms(r, nrm_ref[...]).astype(_bf16)
    z = _dot(h, win_ref[...]) + bin_ref[...]
    y = z[:, 0:C_WIDTH] * jax.nn.sigmoid(z[:, C_WIDTH:2 * C_WIDTH])

    n = ts // C_PHASES
    base = C_HALO - (C_CONV - 1)
    for s in range(n_slabs):
        ybuf_ref[s, C_HALO:C_HALO + ts, :] = y[:, s * LANES:(s + 1) * LANES]
    for s in range(n_slabs):
        slab = ybuf_ref.at[s]
        lanes = slice(s * LANES, (s + 1) * LANES)
        taps = [jnp.broadcast_to(dww_ref[k:k + 1, lanes], (C_ROWS, LANES)) for k in range(C_CONV)]
        bias = jnp.broadcast_to(dwb_ref[:, lanes], (C_ROWS, LANES))
        for mb in range(n // C_ROWS):
            accs = [bias] * C_PHASES
            for i in range(C_PHASES + C_CONV - 1):
                yi = slab[pl.ds(base + i + C_PHASES * C_ROWS * mb, C_ROWS, stride=C_PHASES), :]
                for ph in range(C_PHASES):
                    k = i - ph
                    if 0 <= k < C_CONV:
                        accs[ph] = accs[ph] + yi * taps[k]
            for ph in range(C_PHASES):
                conv_ref[ph * n + mb * C_ROWS:ph * n + (mb + 1) * C_ROWS, lanes] = accs[ph]
        slab[0:C_HALO, :] = slab[ts:ts + C_HALO, :]

    y = _layernorm(conv_ref[...], lng_ref[...], lnb_ref[...])
    y = _natural_rows(y * jax.nn.sigmoid(y), unperm_ref, C_PHASES).astype(_bf16)
    o_ref[...] = r + _dot(y, wout_ref[...]) + bout_ref[...]


def _ffn_kernel(r_ref, p_ref, nrm_ref, wup_ref, dww_ref, dwb_ref, wdn_ref, plen_ref, wg_ref, wp_ref,
                fin_ref, o_ref, carry_ref, buf_ref, unperm_ref, *, final):
    ts = r_ref.shape[0]

    @pl.when(pl.program_id(1) == 0)
    def _():
        carry_ref[...] = jnp.zeros_like(carry_ref)

    r = r_ref[...]
    h = _rms(r, nrm_ref[...]).astype(_bf16)
    n_slabs = FF_CHUNK // LANES

    def up(c):
        for half in range(2):
            col0 = half * D_FF + c * FF_CHUNK
            z = _dot(h, wup_ref[:, col0:col0 + FF_CHUNK])
            for s in range(n_slabs):
                slab = buf_ref.at[c % FF_SLOTS, half, s]
                lo = col0 + s * LANES
                slab[0:SUBLANES, :] = carry_ref[:, lo:lo + LANES]
                slab[SUBLANES:SUBLANES + ts, :] = z[:, s * LANES:(s + 1) * LANES]

    def gated(c):
        cols = []
        for s in range(n_slabs):
            convs = []
            for half in range(2):
                lo = half * D_FF + c * FF_CHUNK + s * LANES
                slab = buf_ref.at[c % FF_SLOTS, half, s]
                carry_ref[:, lo:lo + LANES] = slab[ts:ts + SUBLANES, :]
                convs.append(_conv3_phased(slab, ts, dww_ref[:, lo:lo + LANES])
                             + dwb_ref[:, lo:lo + LANES])
            g, u = convs
            cols.append(g * jax.nn.sigmoid(g) * u)
        return jnp.concatenate(cols, axis=1).astype(_bf16)

    for c in range(FF_SLOTS - 1):
        up(c)
    acc = None
    for c in range(N_FF_CHUNKS):
        if c + FF_SLOTS - 1 < N_FF_CHUNKS:
            up(c + FF_SLOTS - 1)
        d = _dot(gated(c), wdn_ref[c * FF_CHUNK:(c + 1) * FF_CHUNK, :])
        acc = d if acc is None else acc + d
    r2 = r + _natural_rows(acc, unperm_ref)

    gate = jax.nn.sigmoid(_dot(_rms(r2, plen_ref[...]).astype(_bf16), wg_ref[...]))
    r3 = r2 + gate * _dot(p_ref[...].astype(_bf16), wp_ref[...])
    if final:
        r3 = _rms(r3, fin_ref[...])
    o_ref[...] = r3


def _rows(n):
    return pl.BlockSpec((None, SEQ_TILE, n), lambda b, j: (b, j, 0))


def _whole(a):
    nd = a.ndim
    return pl.BlockSpec(a.shape, lambda b, j: (0,) * nd, pipeline_mode=pl.Buffered(1))


def _call(body, name, batch, seq, row_inputs, params, scratch):
    grid = (batch, seq // SEQ_TILE)
    return pl.pallas_call(
        body,
        name=name,
        grid=grid,
        in_specs=[_rows(a.shape[-1]) for a in row_inputs] + [_whole(a) for a in params],
        out_specs=_rows(D_MODEL),
        out_shape=jax.ShapeDtypeStruct((batch, seq, D_MODEL), _f32),
        scratch_shapes=scratch,
        compiler_params=pltpu.CompilerParams(
            dimension_semantics=("arbitrary", "arbitrary"),
            vmem_limit_bytes=VMEM_LIMIT_BYTES),
    )(*row_inputs, *params)


def _row(v):
    return v.reshape(1, -1).astype(_f32)


def _even_layer(r, norm, w_in, a_ws, a_bs, a_ln_g, a_ln_b, b_conv_w, w_out):
    batch, seq, _ = r.shape
    params = [_row(norm), w_in.astype(_bf16), a_ws, a_bs.T, _row(a_ln_g), _row(a_ln_b), b_conv_w,
              w_out.astype(_bf16)]
    scratch = [pltpu.VMEM((B_WIDTH // LANES, SUBLANES + SEQ_TILE, LANES), _f32),
               pltpu.VMEM((B_WIDTH // LANES, SEQ_TILE, LANES), _f32)]
    return _call(_even_kernel, "even_mixer", batch, seq, [r], params, scratch)


def _odd_layer(r, norm, w_in, b_in, dw_w, dw_b, ln_g, ln_b, w_out, b_out):
    batch, seq, _ = r.shape
    params = [_row(norm), w_in.astype(_bf16), _row(b_in), dw_w, _row(dw_b), _row(ln_g), _row(ln_b),
              w_out.astype(_bf16), _row(b_out)]
    scratch = [pltpu.VMEM((C_WIDTH // LANES, C_HALO + SEQ_TILE, LANES), _f32),
               pltpu.VMEM((SEQ_TILE, C_WIDTH), _f32),
               pltpu.VMEM((C_WIDTH // LANES, SEQ_TILE, LANES), _f32)]
    return _call(_odd_kernel, "odd_mixer", batch, seq, [r], params, scratch)


def _ffn_layer(r, p, norm, w_up, dw_w, dw_b, w_down, ple_norm, w_g, w_p, final_norm, final):
    batch, seq, _ = r.shape
    params = [_row(norm), w_up.astype(_bf16), dw_w, _row(dw_b), w_down.astype(_bf16), _row(ple_norm),
              w_g.astype(_bf16), w_p.astype(_bf16), _row(final_norm)]
    scratch = [pltpu.VMEM((SUBLANES, 2 * D_FF), _f32),
               pltpu.VMEM((FF_SLOTS, 2, FF_CHUNK // LANES, SUBLANES + SEQ_TILE, LANES), _f32),
               pltpu.VMEM((D_MODEL // LANES, SEQ_TILE, LANES), _f32)]
    return _call(functools.partial(_ffn_kernel, final=final), "ffn_final" if final else "ffn",
                 batch, seq, [r, p], params, scratch)


def kernel(x, p, ev_norm, ev_w_in, ev_a_ws, ev_a_bs, ev_a_ln_g, ev_a_ln_b, ev_b_conv_w, ev_w_out,
           od_norm, od_w_in, od_b_in, od_dw_w, od_dw_b, od_ln_g, od_ln_b, od_w_out, od_b_out,
           ffn_norm, ffn_w_up, ffn_dw_w, ffn_dw_b, ffn_w_down, ple_w_p, ple_norm, ple_w_g,
           final_norm):
    depth = p.shape[0]
    r = x
    for i in range(depth):
        j = i // 2
        if i % 2 == 0:
            r = _even_layer(r, ev_norm[j], ev_w_in[j], ev_a_ws[j], ev_a_bs[j], ev_a_ln_g[j],
                            ev_a_ln_b[j], ev_b_conv_w[j], ev_w_out[j])
        else:
            r = _odd_layer(r, od_norm[j], od_w_in[j], od_b_in[j], od_dw_w[j], od_dw_b[j],
                           od_ln_g[j], od_ln_b[j], od_w_out[j], od_b_out[j])
        r = _ffn_layer(r, p[i], ffn_norm[i], ffn_w_up[i], ffn_dw_w[i], ffn_dw_b[i], ffn_w_down[i],
                       ple_norm[i], ple_w_g[i], ple_w_p[i], final_norm, final=(i == depth - 1))
    return r
```

```python
import functools

import jax
import jax.numpy as jnp
from jax import lax
from jax.experimental import pallas as pl
from jax.experimental.pallas import tpu as pltpu

D_MODEL = 1024
PLE_DIM = 256
CHUNK = 128
A_HEADS = 4
A_HEAD_DIM = 128
A_WIDTH = A_HEADS * A_HEAD_DIM
B_WIDTH = 512
MIX_WIDTH = A_WIDTH + B_WIDTH
IN_WIDTH = 2 * A_WIDTH + 3 * B_WIDTH
C_WIDTH = D_MODEL
C_CONV = 31
D_FF = 2816
FFN_CONV = 3
EPS = 1e-6

SUBLANES = 8
LANES = 128
ROW_PHASES = 2
SEQ_TILE = 256
SEQ_BLOCK = 1024
FF_CHUNK = 256
N_FF_CHUNKS = D_FF // FF_CHUNK
FF_SLOTS = 4
C_HALO = 32
C_PHASES = 4
C_ROWS = 64
VMEM_LIMIT_BYTES = 56 * 1024 * 1024

assert D_FF % FF_CHUNK == 0 and SEQ_TILE % CHUNK == 0 and C_HALO >= C_CONV - 1
assert SEQ_BLOCK % SEQ_TILE == 0

_bf16 = jnp.bfloat16
_f32 = jnp.float32


def _dot(a, b):
    return jnp.dot(a, b, preferred_element_type=_f32)


def _rms(x, g):
    ms = jnp.mean(x * x, axis=-1, keepdims=True)
    return x * lax.rsqrt(ms + EPS) * g


def _layernorm(x, g, b):
    mu = jnp.mean(x, axis=-1, keepdims=True)
    xc = x - mu
    var = jnp.mean(xc * xc, axis=-1, keepdims=True)
    return xc * lax.rsqrt(var + EPS) * g + b


def _for_each_tile(n_rows, tile):
    def body(i, carry):
        tile(pl.ds(pl.multiple_of(i * SEQ_TILE, SEQ_TILE), SEQ_TILE))
        return carry

    lax.fori_loop(0, n_rows // SEQ_TILE, body, 0)


def _conv3_phased(slab_ref, t, w):
    n = t // ROW_PHASES
    base = SUBLANES - (FFN_CONV - 1)
    taps = [slab_ref[pl.ds(base + i, n, stride=ROW_PHASES), :]
            for i in range(ROW_PHASES + FFN_CONV - 1)]
    return jnp.concatenate(
        [sum(taps[ph + k] * w[k:k + 1, :] for k in range(FFN_CONV)) for ph in range(ROW_PHASES)],
        axis=0)


def _natural_rows(x, slabs_ref, phases=ROW_PHASES):
    t = x.shape[0]
    n = t // phases
    n_slabs = x.shape[1] // LANES
    for s in range(n_slabs):
        for ph in range(phases):
            slabs_ref[s, pl.ds(ph, n, stride=phases), :] = (
                x[ph * n:(ph + 1) * n, s * LANES:(s + 1) * LANES])
    return jnp.concatenate([slabs_ref[s] for s in range(n_slabs)], axis=1)


def _even_tile(r_ref, nrm_ref, win_ref, ws_ref, bs_ref, lng_ref, lnb_ref, cw_ref, wout_ref,
               o_ref, buf_ref, unperm_ref):
    ts = r_ref.shape[0]
    n_chunks = ts // CHUNK

    r = r_ref[...]
    h = _rms(r, nrm_ref[...]).astype(_bf16)
    z = _dot(h, win_ref[...])
    a_u = jax.nn.gelu(z[:, 0:A_WIDTH])
    a_v = jax.nn.gelu(z[:, A_WIDTH:2 * A_WIDTH])
    b_b = z[:, 2 * A_WIDTH:2 * A_WIDTH + B_WIDTH]
    b_c = z[:, 2 * A_WIDTH + B_WIDTH:2 * A_WIDTH + 2 * B_WIDTH]
    b_h = z[:, 2 * A_WIDTH + 2 * B_WIDTH:IN_WIDTH]

    row = lax.broadcasted_iota(jnp.int32, (CHUNK, CHUNK), 0)
    col = lax.broadcasted_iota(jnp.int32, (CHUNK, CHUNK), 1)
    tril = row >= col
    a_parts = []
    for hd in range(A_HEADS):
        lo = hd * A_HEAD_DIM
        vh = _layernorm(a_v[:, lo:lo + A_HEAD_DIM], lng_ref[:, lo:lo + A_HEAD_DIM],
                        lnb_ref[:, lo:lo + A_HEAD_DIM]).astype(_bf16)
        rhs = jnp.concatenate([vh[n * CHUNK:(n + 1) * CHUNK, :] for n in range(n_chunks)], axis=1)
        w = jnp.where(tril, ws_ref[hd], 0.0).astype(_bf16)
        mixed = _dot(w, rhs) + bs_ref[:, hd:hd + 1]
        mixed = jnp.concatenate(
            [mixed[:, n * A_HEAD_DIM:(n + 1) * A_HEAD_DIM] for n in range(n_chunks)], axis=0)
        a_parts.append(a_u[:, lo:lo + A_HEAD_DIM] * mixed)

    x = b_c * b_h
    convs = []
    for s in range(B_WIDTH // LANES):
        slab = buf_ref.at[s]
        slab[SUBLANES:SUBLANES + ts, :] = x[:, s * LANES:(s + 1) * LANES]
        convs.append(_conv3_phased(slab, ts, cw_ref[:, s * LANES:(s + 1) * LANES]))
        slab[0:SUBLANES, :] = slab[ts:ts + SUBLANES, :]
    b_out = b_b * _natural_rows(jnp.concatenate(convs, axis=1), unperm_ref)

    mix = jnp.concatenate(a_parts + [b_out], axis=1).astype(_bf16)
    o_ref[...] = r + _dot(mix, wout_ref[...])


def _even_kernel(r_ref, *rest):
    params, o_ref, (buf_ref, unperm_ref) = rest[:-3], rest[-3], rest[-2:]

    @pl.when(pl.program_id(1) == 0)
    def _():
        buf_ref[:, 0:SUBLANES, :] = jnp.zeros((B_WIDTH // LANES, SUBLANES, LANES), _f32)

    _for_each_tile(r_ref.shape[0], lambda rows: _even_tile(
        r_ref.at[rows], *params, o_ref.at[rows], buf_ref, unperm_ref))


def _odd_tile(r_ref, nrm_ref, win_ref, bin_ref, dww_ref, dwb_ref, lng_ref, lnb_ref, wout_ref,
              bout_ref, o_ref, ybuf_ref, conv_ref, unperm_ref):
    ts = r_ref.shape[0]
    n_slabs = C_WIDTH // LANES

    r = r_ref[...]
    h = _rms(r, nrm_ref[...]).astype(_bf16)
    z = _dot(h, win_ref[...]) + bin_ref[...]
    y = z[:, 0:C_WIDTH] * jax.nn.sigmoid(z[:, C_WIDTH:2 * C_WIDTH])

    n = ts // C_PHASES
    base = C_HALO - (C_CONV - 1)
    for s in range(n_slabs):
        ybuf_ref[s, C_HALO:C_HALO + ts, :] = y[:, s * LANES:(s + 1) * LANES]
    for s in range(n_slabs):
        slab = ybuf_ref.at[s]
        lanes = slice(s * LANES, (s + 1) * LANES)
        taps = [jnp.broadcast_to(dww_ref[k:k + 1, lanes], (C_ROWS, LANES)) for k in range(C_CONV)]
        bias = jnp.broadcast_to(dwb_ref[:, lanes], (C_ROWS, LANES))
        for mb in range(n // C_ROWS):
            accs = [bias] * C_PHASES
            for i in range(C_PHASES + C_CONV - 1):
                yi = slab[pl.ds(base + i + C_PHASES * C_ROWS * mb, C_ROWS, stride=C_PHASES), :]
                for ph in range(C_PHASES):
                    k = i - ph
                    if 0 <= k < C_CONV:
                        accs[ph] = accs[ph] + yi * taps[k]
            for ph in range(C_PHASES):
                conv_ref[ph * n + mb * C_ROWS:ph * n + (mb + 1) * C_ROWS, lanes] = accs[ph]
        slab[0:C_HALO, :] = slab[ts:ts + C_HALO, :]

    y = _layernorm(conv_ref[...], lng_ref[...], lnb_ref[...])
    y = _natural_rows(y * jax.nn.sigmoid(y), unperm_ref, C_PHASES).astype(_bf16)
    o_ref[...] = r + _dot(y, wout_ref[...]) + bout_ref[...]


def _odd_kernel(r_ref, *rest):
    params, o_ref, (ybuf_ref, conv_ref, unperm_ref) = rest[:-4], rest[-4], rest[-3:]

    @pl.when(pl.program_id(1) == 0)
    def _():
        ybuf_ref[:, 0:C_HALO, :] = jnp.zeros((C_WIDTH // LANES, C_HALO, LANES), _f32)

    _for_each_tile(r_ref.shape[0], lambda rows: _odd_tile(
        r_ref.at[rows], *params, o_ref.at[rows], ybuf_ref, conv_ref, unperm_ref))


def _ffn_tile(r_ref, p_ref, nrm_ref, wup_ref, dww_ref, dwb_ref, wdn_ref, plen_ref, wg_ref, wp_ref,
              fin_ref, o_ref, carry_ref, buf_ref, unperm_ref, *, final):
    ts = r_ref.shape[0]

    r = r_ref[...]
    h = _rms(r, nrm_ref[...]).astype(_bf16)
    n_slabs = FF_CHUNK // LANES

    def up(c):
        for half in range(2):
            col0 = half * D_FF + c * FF_CHUNK
            z = _dot(h, wup_ref[:, col0:col0 + FF_CHUNK])
            for s in range(n_slabs):
                slab = buf_ref.at[c % FF_SLOTS, half, s]
                lo = col0 + s * LANES
                slab[0:SUBLANES, :] = carry_ref[:, lo:lo + LANES]
                slab[SUBLANES:SUBLANES + ts, :] = z[:, s * LANES:(s + 1) * LANES]

    def gated(c):
        cols = []
        for s in range(n_slabs):
            convs = []
            for half in range(2):
                lo = half * D_FF + c * FF_CHUNK + s * LANES
                slab = buf_ref.at[c % FF_SLOTS, half, s]
                carry_ref[:, lo:lo + LANES] = slab[ts:ts + SUBLANES, :]
                convs.append(_conv3_phased(slab, ts, dww_ref[:, lo:lo + LANES])
                             + dwb_ref[:, lo:lo + LANES])
            g, u = convs
            cols.append(g * jax.nn.sigmoid(g) * u)
        return jnp.concatenate(cols, axis=1).astype(_bf16)

    for c in range(FF_SLOTS - 1):
        up(c)
    acc = None
    for c in range(N_FF_CHUNKS):
        if c + FF_SLOTS - 1 < N_FF_CHUNKS:
            up(c + FF_SLOTS - 1)
        d = _dot(gated(c), wdn_ref[c * FF_CHUNK:(c + 1) * FF_CHUNK, :])
        acc = d if acc is None else acc + d
    r2 = r + _natural_rows(acc, unperm_ref)

    gate = jax.nn.sigmoid(_dot(_rms(r2, plen_ref[...]).astype(_bf16), wg_ref[...]))
    r3 = r2 + gate * _dot(p_ref[...].astype(_bf16), wp_ref[...])
    if final:
        r3 = _rms(r3, fin_ref[...])
    o_ref[...] = r3


def _ffn_kernel(r_ref, p_ref, *rest, final):
    params, o_ref, (carry_ref, buf_ref, unperm_ref) = rest[:-4], rest[-4], rest[-3:]

    @pl.when(pl.program_id(1) == 0)
    def _():
        carry_ref[...] = jnp.zeros_like(carry_ref)

    _for_each_tile(r_ref.shape[0], lambda rows: _ffn_tile(
        r_ref.at[rows], p_ref.at[rows], *params, o_ref.at[rows], carry_ref, buf_ref, unperm_ref,
        final=final))


def _rows(n):
    return pl.BlockSpec((None, SEQ_BLOCK, n), lambda b, j: (b, j, 0))


def _whole(a):
    nd = a.ndim
    return pl.BlockSpec(a.shape, lambda b, j: (0,) * nd, pipeline_mode=pl.Buffered(1))


def _call(body, name, batch, seq, row_inputs, params, scratch):
    grid = (batch, seq // SEQ_BLOCK)
    return pl.pallas_call(
        body,
        name=name,
        grid=grid,
        in_specs=[_rows(a.shape[-1]) for a in row_inputs] + [_whole(a) for a in params],
        out_specs=_rows(D_MODEL),
        out_shape=jax.ShapeDtypeStruct((batch, seq, D_MODEL), _f32),
        scratch_shapes=scratch,
        compiler_params=pltpu.CompilerParams(
            dimension_semantics=("arbitrary", "arbitrary"),
            vmem_limit_bytes=VMEM_LIMIT_BYTES),
    )(*row_inputs, *params)


def _row(v):
    return v.reshape(1, -1).astype(_f32)


def _even_layer(r, norm, w_in, a_ws, a_bs, a_ln_g, a_ln_b, b_conv_w, w_out):
    batch, seq, _ = r.shape
    params = [_row(norm), w_in.astype(_bf16), a_ws, a_bs.T, _row(a_ln_g), _row(a_ln_b), b_conv_w,
              w_out.astype(_bf16)]
    scratch = [pltpu.VMEM((B_WIDTH // LANES, SUBLANES + SEQ_TILE, LANES), _f32),
               pltpu.VMEM((B_WIDTH // LANES, SEQ_TILE, LANES), _f32)]
    return _call(_even_kernel, "even_mixer", batch, seq, [r], params, scratch)


def _odd_layer(r, norm, w_in, b_in, dw_w, dw_b, ln_g, ln_b, w_out, b_out):
    batch, seq, _ = r.shape
    params = [_row(norm), w_in.astype(_bf16), _row(b_in), dw_w, _row(dw_b), _row(ln_g), _row(ln_b),
              w_out.astype(_bf16), _row(b_out)]
    scratch = [pltpu.VMEM((C_WIDTH // LANES, C_HALO + SEQ_TILE, LANES), _f32),
               pltpu.VMEM((SEQ_TILE, C_WIDTH), _f32),
               pltpu.VMEM((C_WIDTH // LANES, SEQ_TILE, LANES), _f32)]
    return _call(_odd_kernel, "odd_mixer", batch, seq, [r], params, scratch)


def _ffn_layer(r, p, norm, w_up, dw_w, dw_b, w_down, ple_norm, w_g, w_p, final_norm, final):
    batch, seq, _ = r.shape
    params = [_row(norm), w_up.astype(_bf16), dw_w, _row(dw_b), w_down.astype(_bf16), _row(ple_norm),
              w_g.astype(_bf16), w_p.astype(_bf16), _row(final_norm)]
    scratch = [pltpu.VMEM((SUBLANES, 2 * D_FF), _f32),
               pltpu.VMEM((FF_SLOTS, 2, FF_CHUNK // LANES, SUBLANES + SEQ_TILE, LANES), _f32),
               pltpu.VMEM((D_MODEL // LANES, SEQ_TILE, LANES), _f32)]
    return _call(functools.partial(_ffn_kernel, final=final), "ffn_final" if final else "ffn",
                 batch, seq, [r, p], params, scratch)


def kernel(x, p, ev_norm, ev_w_in, ev_a_ws, ev_a_bs, ev_a_ln_g, ev_a_ln_b, ev_b_conv_w, ev_w_out,
           od_norm, od_w_in, od_b_in, od_dw_w, od_dw_b, od_ln_g, od_ln_b, od_w_out, od_b_out,
           ffn_norm, ffn_w_up, ffn_dw_w, ffn_dw_b, ffn_w_down, ple_w_p, ple_norm, ple_w_g,
           final_norm):
    depth = p.shape[0]
    r = x
    for i in range(depth):
        j = i // 2
        if i % 2 == 0:
            r = _even_layer(r, ev_norm[j], ev_w_in[j], ev_a_ws[j], ev_a_bs[j], ev_a_ln_g[j],
                            ev_a_ln_b[j], ev_b_conv_w[j], ev_w_out[j])
        else:
            r = _odd_layer(r, od_norm[j], od_w_in[j], od_b_in[j], od_dw_w[j], od_dw_b[j],
                           od_ln_g[j], od_ln_b[j], od_w_out[j], od_b_out[j])
        r = _ffn_layer(r, p[i], ffn_norm[i], ffn_w_up[i], ffn_dw_w[i], ffn_dw_b[i], ffn_w_down[i],
                       ple_norm[i], ple_w_g[i], ple_w_p[i], final_norm, final=(i == depth - 1))
    return r
```

```python
import functools

import jax
import jax.numpy as jnp
from jax import lax
from jax.experimental import pallas as pl
from jax.experimental.pallas import tpu as pltpu

D_MODEL = 1024
PLE_DIM = 256
CHUNK = 128
A_HEADS = 4
A_HEAD_DIM = 128
A_WIDTH = A_HEADS * A_HEAD_DIM
B_WIDTH = 512
MIX_WIDTH = A_WIDTH + B_WIDTH
IN_WIDTH = 2 * A_WIDTH + 3 * B_WIDTH
C_WIDTH = D_MODEL
C_CONV = 31
D_FF = 2816
FFN_CONV = 3
EPS = 1e-6

SUBLANES = 8
LANES = 128
ROW_PHASES = 2
SEQ_TILE = 512
SEQ_BLOCK = 1024
FF_CHUNK = 256
N_FF_CHUNKS = D_FF // FF_CHUNK
FF_SLOTS = 4
C_HALO = 32
C_PHASES = 4
C_ROWS = 64
C_GROUP = 256
C_TILES = 1
VMEM_LIMIT_BYTES = 56 * 1024 * 1024

assert D_FF % FF_CHUNK == 0 and SEQ_TILE % CHUNK == 0 and C_HALO >= C_CONV - 1
assert SEQ_BLOCK % SEQ_TILE == 0

_bf16 = jnp.bfloat16
_f32 = jnp.float32


def _dot(a, b):
    return jnp.dot(a, b, preferred_element_type=_f32)


def _rms(x, g):
    ms = jnp.mean(x * x, axis=-1, keepdims=True)
    return x * lax.rsqrt(ms + EPS) * g


def _layernorm(x, g, b):
    mu = jnp.mean(x, axis=-1, keepdims=True)
    xc = x - mu
    var = jnp.mean(xc * xc, axis=-1, keepdims=True)
    return xc * lax.rsqrt(var + EPS) * g + b


def _for_each_tile(n_rows, tile, tile_rows=SEQ_TILE):
    def body(i, carry):
        tile(pl.ds(pl.multiple_of(i * tile_rows, tile_rows), tile_rows))
        return carry

    lax.fori_loop(0, n_rows // tile_rows, body, 0)


def _conv3_phased(slab_ref, t, w):
    n = t // ROW_PHASES
    base = SUBLANES - (FFN_CONV - 1)
    taps = [slab_ref[pl.ds(base + i, n, stride=ROW_PHASES), :]
            for i in range(ROW_PHASES + FFN_CONV - 1)]
    return jnp.concatenate(
        [sum(taps[ph + k] * w[k:k + 1, :] for k in range(FFN_CONV)) for ph in range(ROW_PHASES)],
        axis=0)


def _natural_rows(x, slabs_ref, phases=ROW_PHASES):
    t = x.shape[0]
    n = t // phases
    n_slabs = x.shape[1] // LANES
    for s in range(n_slabs):
        for ph in range(phases):
            slabs_ref[s, pl.ds(ph, n, stride=phases), :] = (
                x[ph * n:(ph + 1) * n, s * LANES:(s + 1) * LANES])
    return jnp.concatenate([slabs_ref[s] for s in range(n_slabs)], axis=1)


def _even_tile(r_ref, nrm_ref, win_ref, ws_ref, bs_ref, lng_ref, lnb_ref, cw_ref, wout_ref,
               o_ref, buf_ref, unperm_ref):
    ts = r_ref.shape[0]
    n_chunks = ts // CHUNK

    r = r_ref[...]
    h = _rms(r, nrm_ref[...]).astype(_bf16)
    z = _dot(h, win_ref[...])
    a_u = jax.nn.gelu(z[:, 0:A_WIDTH])
    a_v = jax.nn.gelu(z[:, A_WIDTH:2 * A_WIDTH])
    b_b = z[:, 2 * A_WIDTH:2 * A_WIDTH + B_WIDTH]
    b_c = z[:, 2 * A_WIDTH + B_WIDTH:2 * A_WIDTH + 2 * B_WIDTH]
    b_h = z[:, 2 * A_WIDTH + 2 * B_WIDTH:IN_WIDTH]

    row = lax.broadcasted_iota(jnp.int32, (CHUNK, CHUNK), 0)
    col = lax.broadcasted_iota(jnp.int32, (CHUNK, CHUNK), 1)
    tril = row >= col
    a_parts = []
    for hd in range(A_HEADS):
        lo = hd * A_HEAD_DIM
        vh = _layernorm(a_v[:, lo:lo + A_HEAD_DIM], lng_ref[:, lo:lo + A_HEAD_DIM],
                        lnb_ref[:, lo:lo + A_HEAD_DIM]).astype(_bf16)
        rhs = jnp.concatenate([vh[n * CHUNK:(n + 1) * CHUNK, :] for n in range(n_chunks)], axis=1)
        w = jnp.where(tril, ws_ref[hd], 0.0).astype(_bf16)
        mixed = _dot(w, rhs) + bs_ref[:, hd:hd + 1]
        mixed = jnp.concatenate(
            [mixed[:, n * A_HEAD_DIM:(n + 1) * A_HEAD_DIM] for n in range(n_chunks)], axis=0)
        a_parts.append(a_u[:, lo:lo + A_HEAD_DIM] * mixed)

    x = b_c * b_h
    convs = []
    for s in range(B_WIDTH // LANES):
        slab = buf_ref.at[s]
        slab[SUBLANES:SUBLANES + ts, :] = x[:, s * LANES:(s + 1) * LANES]
        convs.append(_conv3_phased(slab, ts, cw_ref[:, s * LANES:(s + 1) * LANES]))
        slab[0:SUBLANES, :] = slab[ts:ts + SUBLANES, :]
    b_out = b_b * _natural_rows(jnp.concatenate(convs, axis=1), unperm_ref)

    mix = jnp.concatenate(a_parts + [b_out], axis=1).astype(_bf16)
    o_ref[...] = r + _dot(mix, wout_ref[...])


def _even_kernel(r_ref, *rest):
    params, o_ref, (buf_ref, unperm_ref) = rest[:-3], rest[-3], rest[-2:]

    @pl.when(pl.program_id(1) == 0)
    def _():
        buf_ref[:, 0:SUBLANES, :] = jnp.zeros((B_WIDTH // LANES, SUBLANES, LANES), _f32)

    _for_each_tile(r_ref.shape[0], lambda rows: _even_tile(
        r_ref.at[rows], *params, o_ref.at[rows], buf_ref, unperm_ref))


def _odd_tile(r_ref, nrm_ref, win_ref, bin_ref, dww_ref, dwb_ref, lng_ref, lnb_ref, wout_ref,
              bout_ref, o_ref, ybuf_ref, conv_ref, unperm_ref):
    ts = SEQ_TILE
    n_slabs = C_WIDTH // LANES
    n_groups = C_WIDTH // C_GROUP
    n = ts // C_PHASES
    base = C_HALO - (C_CONV - 1)

    def glu(t, h, q):
        z = (_dot(h, win_ref[:, 2 * q * C_GROUP:2 * (q + 1) * C_GROUP])
             + bin_ref[:, 2 * q * C_GROUP:2 * (q + 1) * C_GROUP])
        y = z[:, 0:C_GROUP] * jax.nn.sigmoid(z[:, C_GROUP:2 * C_GROUP])
        for j in range(C_GROUP // LANES):
            s = q * (C_GROUP // LANES) + j
            ybuf_ref[s, C_HALO + t * ts:C_HALO + (t + 1) * ts, :] = y[:, j * LANES:(j + 1) * LANES]

    def conv(t, s):
        slab = ybuf_ref.at[s]
        lanes = slice(s * LANES, (s + 1) * LANES)
        taps = [jnp.broadcast_to(dww_ref[k:k + 1, lanes], (C_ROWS, LANES)) for k in range(C_CONV)]
        bias = jnp.broadcast_to(dwb_ref[:, lanes], (C_ROWS, LANES))
        for mb in range(n // C_ROWS):
            accs = [bias] * C_PHASES
            for i in range(C_PHASES + C_CONV - 1):
                yi = slab[pl.ds(base + t * ts + i + C_PHASES * C_ROWS * mb, C_ROWS,
                                stride=C_PHASES), :]
                for ph in range(C_PHASES):
                    k = i - ph
                    if 0 <= k < C_CONV:
                        accs[ph] = accs[ph] + yi * taps[k]
            for ph in range(C_PHASES):
                conv_ref[t, ph * n + mb * C_ROWS:ph * n + (mb + 1) * C_ROWS, lanes] = accs[ph]

    def finish(t):
        y = _layernorm(conv_ref[t], lng_ref[...], lnb_ref[...])
        y = _natural_rows(y * jax.nn.sigmoid(y), unperm_ref.at[t], C_PHASES).astype(_bf16)
        o_ref[t * ts:(t + 1) * ts, :] = (r_ref[t * ts:(t + 1) * ts, :] + _dot(y, wout_ref[...])
                                         + bout_ref[...])

    for t in range(C_TILES):
        h = _rms(r_ref[t * ts:(t + 1) * ts, :], nrm_ref[...]).astype(_bf16)
        glu(t, h, 0)
        for q in range(n_groups):
            if q + 1 < n_groups:
                glu(t, h, q + 1)
            if q == 0 and t > 0:
                finish(t - 1)
            for j in range(C_GROUP // LANES):
                conv(t, q * (C_GROUP // LANES) + j)
    finish(C_TILES - 1)
    for s in range(n_slabs):
        ybuf_ref[s, 0:C_HALO, :] = ybuf_ref[s, C_TILES * ts:C_TILES * ts + C_HALO, :]


def _odd_kernel(r_ref, *rest):
    params, o_ref, (ybuf_ref, conv_ref, unperm_ref) = rest[:-4], rest[-4], rest[-3:]

    @pl.when(pl.program_id(1) == 0)
    def _():
        ybuf_ref[:, 0:C_HALO, :] = jnp.zeros((C_WIDTH // LANES, C_HALO, LANES), _f32)

    _for_each_tile(r_ref.shape[0], lambda rows: _odd_tile(
        r_ref.at[rows], *params, o_ref.at[rows], ybuf_ref, conv_ref, unperm_ref),
        tile_rows=C_TILES * SEQ_TILE)


def _ffn_tile(r_ref, p_ref, nrm_ref, wup_ref, dww_ref, dwb_ref, wdn_ref, plen_ref, wg_ref, wp_ref,
              fin_ref, o_ref, carry_ref, buf_ref, unperm_ref, *, final):
    ts = r_ref.shape[0]

    pe = _dot(p_ref[...].astype(_bf16), wp_ref[...])
    r = r_ref[...]
    h = _rms(r, nrm_ref[...]).astype(_bf16)
    n_slabs = FF_CHUNK // LANES

    def up(c):
        col0 = 2 * c * FF_CHUNK
        z = _dot(h, wup_ref[:, col0:col0 + 2 * FF_CHUNK])
        for half in range(2):
            for s in range(n_slabs):
                slab = buf_ref.at[c % FF_SLOTS, half, s]
                lo = col0 + half * FF_CHUNK + s * LANES
                slab[0:SUBLANES, :] = carry_ref[:, lo:lo + LANES]
                slab[SUBLANES:SUBLANES + ts, :] = z[:, lo - col0:lo - col0 + LANES]

    def gated(c):
        cols = []
        for s in range(n_slabs):
            convs = []
            for half in range(2):
                lo = (2 * c + half) * FF_CHUNK + s * LANES
                slab = buf_ref.at[c % FF_SLOTS, half, s]
                carry_ref[:, lo:lo + LANES] = slab[ts:ts + SUBLANES, :]
                convs.append(_conv3_phased(slab, ts, dww_ref[:, lo:lo + LANES])
                             + dwb_ref[:, lo:lo + LANES])
            g, u = convs
            cols.append(g * jax.nn.sigmoid(g) * u)
        return jnp.concatenate(cols, axis=1).astype(_bf16)

    for c in range(FF_SLOTS - 1):
        up(c)
    acc = None
    for c in range(N_FF_CHUNKS):
        if c + FF_SLOTS - 1 < N_FF_CHUNKS:
            up(c + FF_SLOTS - 1)
        d = _dot(gated(c), wdn_ref[c * FF_CHUNK:(c + 1) * FF_CHUNK, :])
        acc = d if acc is None else acc + d
    r2 = r + _natural_rows(acc, unperm_ref)

    gate = jax.nn.sigmoid(_dot(_rms(r2, plen_ref[...]).astype(_bf16), wg_ref[...]))
    r3 = r2 + gate * pe
    if final:
        r3 = _rms(r3, fin_ref[...])
    o_ref[...] = r3


def _ffn_kernel(r_ref, p_ref, *rest, final):
    params, o_ref, (carry_ref, buf_ref, unperm_ref) = rest[:-4], rest[-4], rest[-3:]

    @pl.when(pl.program_id(1) == 0)
    def _():
        carry_ref[...] = jnp.zeros_like(carry_ref)

    _for_each_tile(r_ref.shape[0], lambda rows: _ffn_tile(
        r_ref.at[rows], p_ref.at[rows], *params, o_ref.at[rows], carry_ref, buf_ref, unperm_ref,
        final=final))


def _rows(n):
    return pl.BlockSpec((None, SEQ_BLOCK, n), lambda b, j: (b, j, 0))


def _whole(a):
    nd = a.ndim
    return pl.BlockSpec(a.shape, lambda b, j: (0,) * nd, pipeline_mode=pl.Buffered(1))


def _call(body, name, batch, seq, row_inputs, params, scratch):
    grid = (batch, seq // SEQ_BLOCK)
    return pl.pallas_call(
        body,
        name=name,
        grid=grid,
        in_specs=[_rows(a.shape[-1]) for a in row_inputs] + [_whole(a) for a in params],
        out_specs=_rows(D_MODEL),
        out_shape=jax.ShapeDtypeStruct((batch, seq, D_MODEL), _f32),
        scratch_shapes=scratch,
        compiler_params=pltpu.CompilerParams(
            dimension_semantics=("arbitrary", "arbitrary"),
            vmem_limit_bytes=VMEM_LIMIT_BYTES),
    )(*row_inputs, *params)


def _row(v):
    return v.reshape(1, -1).astype(_f32)


def _group_columns(a, width):
    lead = a.shape[:-1]
    groups = a.shape[-1] // (2 * width)
    return a.reshape(*lead, 2, groups, width).swapaxes(-3, -2).reshape(*lead, 2 * groups * width)


def _even_layer(r, norm, w_in, a_ws, a_bs, a_ln_g, a_ln_b, b_conv_w, w_out):
    batch, seq, _ = r.shape
    params = [_row(norm), w_in.astype(_bf16), a_ws, a_bs.T, _row(a_ln_g), _row(a_ln_b), b_conv_w,
              w_out.astype(_bf16)]
    scratch = [pltpu.VMEM((B_WIDTH // LANES, SUBLANES + SEQ_TILE, LANES), _f32),
               pltpu.VMEM((B_WIDTH // LANES, SEQ_TILE, LANES), _f32)]
    return _call(_even_kernel, "even_mixer", batch, seq, [r], params, scratch)


def _odd_layer(r, norm, w_in, b_in, dw_w, dw_b, ln_g, ln_b, w_out, b_out):
    batch, seq, _ = r.shape
    params = [_row(norm), _group_columns(w_in.astype(_bf16), C_GROUP),
              _group_columns(_row(b_in), C_GROUP), dw_w, _row(dw_b), _row(ln_g), _row(ln_b),
              w_out.astype(_bf16), _row(b_out)]
    scratch = [pltpu.VMEM((C_WIDTH // LANES, C_HALO + C_TILES * SEQ_TILE, LANES), _f32),
               pltpu.VMEM((C_TILES, SEQ_TILE, C_WIDTH), _f32),
               pltpu.VMEM((C_TILES, C_WIDTH // LANES, SEQ_TILE, LANES), _f32)]
    return _call(_odd_kernel, "odd_mixer", batch, seq, [r], params, scratch)


def _ffn_layer(r, p, norm, w_up, dw_w, dw_b, w_down, ple_norm, w_g, w_p, final_norm, final):
    batch, seq, _ = r.shape
    params = [_row(norm), _group_columns(w_up.astype(_bf16), FF_CHUNK),
              _group_columns(dw_w, FF_CHUNK), _group_columns(_row(dw_b), FF_CHUNK),
              w_down.astype(_bf16), _row(ple_norm), w_g.astype(_bf16), w_p.astype(_bf16),
              _row(final_norm)]
    scratch = [pltpu.VMEM((SUBLANES, 2 * D_FF), _f32),
               pltpu.VMEM((FF_SLOTS, 2, FF_CHUNK // LANES, SUBLANES + SEQ_TILE, LANES), _f32),
               pltpu.VMEM((D_MODEL // LANES, SEQ_TILE, LANES), _f32)]
    return _call(functools.partial(_ffn_kernel, final=final), "ffn_final" if final else "ffn",
                 batch, seq, [r, p], params, scratch)


def kernel(x, p, ev_norm, ev_w_in, ev_a_ws, ev_a_bs, ev_a_ln_g, ev_a_ln_b, ev_b_conv_w, ev_w_out,
           od_norm, od_w_in, od_b_in, od_dw_w, od_dw_b, od_ln_g, od_ln_b, od_w_out, od_b_out,
           ffn_norm, ffn_w_up, ffn_dw_w, ffn_dw_b, ffn_w_down, ple_w_p, ple_norm, ple_w_g,
           final_norm):
    depth = p.shape[0]
    r = x
    for i in range(depth):
        j = i // 2
        if i % 2 == 0:
            r = _even_layer(r, ev_norm[j], ev_w_in[j], ev_a_ws[j], ev_a_bs[j], ev_a_ln_g[j],
                            ev_a_ln_b[j], ev_b_conv_w[j], ev_w_out[j])
        else:
            r = _odd_layer(r, od_norm[j], od_w_in[j], od_b_in[j], od_dw_w[j], od_dw_b[j],
                           od_ln_g[j], od_ln_b[j], od_w_out[j], od_b_out[j])
        r = _ffn_layer(r, p[i], ffn_norm[i], ffn_w_up[i], ffn_dw_w[i], ffn_dw_b[i], ffn_w_down[i],
                       ple_norm[i], ple_w_g[i], ple_w_p[i], final_norm, final=(i == depth - 1))
    return r
```

```python
import functools

import jax
import jax.numpy as jnp
from jax import lax
from jax.experimental import pallas as pl
from jax.experimental.pallas import tpu as pltpu

D_MODEL = 1024
PLE_DIM = 256
CHUNK = 128
A_HEADS = 4
A_HEAD_DIM = 128
A_WIDTH = A_HEADS * A_HEAD_DIM
B_WIDTH = 512
MIX_WIDTH = A_WIDTH + B_WIDTH
IN_WIDTH = 2 * A_WIDTH + 3 * B_WIDTH
C_WIDTH = D_MODEL
C_CONV = 31
D_FF = 2816
FFN_CONV = 3
EPS = 1e-6

SUBLANES = 8
LANES = 128
ROW_PHASES = 2
SEQ_TILE = 512
SEQ_BLOCK = 1024
FF_CHUNK = 256
N_FF_CHUNKS = D_FF // FF_CHUNK
FF_SLOTS = 4
C_HALO = 32
C_PHASES = 4
C_ROWS = 64
C_GROUP = 256
C_TILES = 1
CAST_STEPS = 8
VMEM_LIMIT_BYTES = 56 * 1024 * 1024

assert D_FF % FF_CHUNK == 0 and SEQ_TILE % CHUNK == 0 and C_HALO >= C_CONV - 1
assert SEQ_BLOCK % SEQ_TILE == 0

_bf16 = jnp.bfloat16
_f32 = jnp.float32


def _dot(a, b):
    return jnp.dot(a, b, preferred_element_type=_f32)


def _rms(x, g):
    ms = jnp.mean(x * x, axis=-1, keepdims=True)
    return x * lax.rsqrt(ms + EPS) * g


def _layernorm(x, g, b):
    mu = jnp.mean(x, axis=-1, keepdims=True)
    xc = x - mu
    var = jnp.mean(xc * xc, axis=-1, keepdims=True)
    return xc * lax.rsqrt(var + EPS) * g + b


def _for_each_tile(n_rows, tile, tile_rows=SEQ_TILE):
    def body(i, carry):
        tile(pl.ds(pl.multiple_of(i * tile_rows, tile_rows), tile_rows))
        return carry

    lax.fori_loop(0, n_rows // tile_rows, body, 0)


def _conv3_phased(slab_ref, t, w):
    n = t // ROW_PHASES
    base = SUBLANES - (FFN_CONV - 1)
    taps = [slab_ref[pl.ds(base + i, n, stride=ROW_PHASES), :]
            for i in range(ROW_PHASES + FFN_CONV - 1)]
    return jnp.concatenate(
        [sum(taps[ph + k] * w[k:k + 1, :] for k in range(FFN_CONV)) for ph in range(ROW_PHASES)],
        axis=0)


def _natural_rows(x, slabs_ref, phases=ROW_PHASES):
    t = x.shape[0]
    n = t // phases
    n_slabs = x.shape[1] // LANES
    for s in range(n_slabs):
        for ph in range(phases):
            slabs_ref[s, pl.ds(ph, n, stride=phases), :] = (
                x[ph * n:(ph + 1) * n, s * LANES:(s + 1) * LANES])
    return jnp.concatenate([slabs_ref[s] for s in range(n_slabs)], axis=1)


def _even_tile(r_ref, nrm_ref, win_ref, ws_ref, bs_ref, lng_ref, lnb_ref, cw_ref, wout_ref,
               o_ref, buf_ref, unperm_ref):
    ts = r_ref.shape[0]
    n_chunks = ts // CHUNK

    r = r_ref[...]
    h = _rms(r, nrm_ref[...]).astype(_bf16)
    z = _dot(h, win_ref[...])
    a_u = jax.nn.gelu(z[:, 0:A_WIDTH])
    a_v = jax.nn.gelu(z[:, A_WIDTH:2 * A_WIDTH])
    b_b = z[:, 2 * A_WIDTH:2 * A_WIDTH + B_WIDTH]
    b_c = z[:, 2 * A_WIDTH + B_WIDTH:2 * A_WIDTH + 2 * B_WIDTH]
    b_h = z[:, 2 * A_WIDTH + 2 * B_WIDTH:IN_WIDTH]

    row = lax.broadcasted_iota(jnp.int32, (CHUNK, CHUNK), 0)
    col = lax.broadcasted_iota(jnp.int32, (CHUNK, CHUNK), 1)
    tril = row >= col
    a_parts = []
    for hd in range(A_HEADS):
        lo = hd * A_HEAD_DIM
        vh = _layernorm(a_v[:, lo:lo + A_HEAD_DIM], lng_ref[:, lo:lo + A_HEAD_DIM],
                        lnb_ref[:, lo:lo + A_HEAD_DIM]).astype(_bf16)
        rhs = jnp.concatenate([vh[n * CHUNK:(n + 1) * CHUNK, :] for n in range(n_chunks)], axis=1)
        w = jnp.where(tril, ws_ref[hd], 0.0).astype(_bf16)
        mixed = _dot(w, rhs) + bs_ref[:, hd:hd + 1]
        mixed = jnp.concatenate(
            [mixed[:, n * A_HEAD_DIM:(n + 1) * A_HEAD_DIM] for n in range(n_chunks)], axis=0)
        a_parts.append(a_u[:, lo:lo + A_HEAD_DIM] * mixed)

    x = b_c * b_h
    convs = []
    for s in range(B_WIDTH // LANES):
        slab = buf_ref.at[s]
        slab[SUBLANES:SUBLANES + ts, :] = x[:, s * LANES:(s + 1) * LANES]
        convs.append(_conv3_phased(slab, ts, cw_ref[:, s * LANES:(s + 1) * LANES]))
        slab[0:SUBLANES, :] = slab[ts:ts + SUBLANES, :]
    b_out = b_b * _natural_rows(jnp.concatenate(convs, axis=1), unperm_ref)

    mix = jnp.concatenate(a_parts + [b_out], axis=1).astype(_bf16)
    o_ref[...] = r + _dot(mix, wout_ref[...])


def _even_kernel(r_ref, *rest):
    params, o_ref, (buf_ref, unperm_ref) = rest[:-3], rest[-3], rest[-2:]

    @pl.when(pl.program_id(1) == 0)
    def _():
        buf_ref[:, 0:SUBLANES, :] = jnp.zeros((B_WIDTH // LANES, SUBLANES, LANES), _f32)

    _for_each_tile(r_ref.shape[0], lambda rows: _even_tile(
        r_ref.at[rows], *params, o_ref.at[rows], buf_ref, unperm_ref))


def _odd_tile(r_ref, nrm_ref, win_ref, bin_ref, dww_ref, dwb_ref, lng_ref, lnb_ref, wout_ref,
              bout_ref, o_ref, ybuf_ref, conv_ref, unperm_ref):
    ts = SEQ_TILE
    n_slabs = C_WIDTH // LANES
    n_groups = C_WIDTH // C_GROUP
    n = ts // C_PHASES
    base = C_HALO - (C_CONV - 1)

    def glu(t, h, q):
        z_v, z_g = [_dot(h, win_ref[:, c0:c0 + C_GROUP]) + bin_ref[:, c0:c0 + C_GROUP]
                    for c0 in (q * C_GROUP, C_WIDTH + q * C_GROUP)]
        y = z_v * jax.nn.sigmoid(z_g)
        for j in range(C_GROUP // LANES):
            s = q * (C_GROUP // LANES) + j
            ybuf_ref[s, C_HALO + t * ts:C_HALO + (t + 1) * ts, :] = y[:, j * LANES:(j + 1) * LANES]

    def conv(t, s):
        slab = ybuf_ref.at[s]
        lanes = slice(s * LANES, (s + 1) * LANES)
        taps = [jnp.broadcast_to(dww_ref[k:k + 1, lanes], (C_ROWS, LANES)) for k in range(C_CONV)]
        bias = jnp.broadcast_to(dwb_ref[:, lanes], (C_ROWS, LANES))
        for mb in range(n // C_ROWS):
            accs = [bias] * C_PHASES
            for i in range(C_PHASES + C_CONV - 1):
                yi = slab[pl.ds(base + t * ts + i + C_PHASES * C_ROWS * mb, C_ROWS,
                                stride=C_PHASES), :]
                for ph in range(C_PHASES):
                    k = i - ph
                    if 0 <= k < C_CONV:
                        accs[ph] = accs[ph] + yi * taps[k]
            for ph in range(C_PHASES):
                conv_ref[t, ph * n + mb * C_ROWS:ph * n + (mb + 1) * C_ROWS, lanes] = accs[ph]

    def finish(t):
        y = _layernorm(conv_ref[t], lng_ref[...], lnb_ref[...])
        y = _natural_rows(y * jax.nn.sigmoid(y), unperm_ref.at[t], C_PHASES).astype(_bf16)
        o_ref[t * ts:(t + 1) * ts, :] = (r_ref[t * ts:(t + 1) * ts, :] + _dot(y, wout_ref[...])
                                         + bout_ref[...])

    for t in range(C_TILES):
        h = _rms(r_ref[t * ts:(t + 1) * ts, :], nrm_ref[...]).astype(_bf16)
        glu(t, h, 0)
        for q in range(n_groups):
            if q + 1 < n_groups:
                glu(t, h, q + 1)
            if q == 0 and t > 0:
                finish(t - 1)
            for j in range(C_GROUP // LANES):
                conv(t, q * (C_GROUP // LANES) + j)
    finish(C_TILES - 1)
    for s in range(n_slabs):
        ybuf_ref[s, 0:C_HALO, :] = ybuf_ref[s, C_TILES * ts:C_TILES * ts + C_HALO, :]


def _odd_kernel(r_ref, *rest):
    params, o_ref, (ybuf_ref, conv_ref, unperm_ref) = rest[:-4], rest[-4], rest[-3:]

    @pl.when(pl.program_id(1) == 0)
    def _():
        ybuf_ref[:, 0:C_HALO, :] = jnp.zeros((C_WIDTH // LANES, C_HALO, LANES), _f32)

    _for_each_tile(r_ref.shape[0], lambda rows: _odd_tile(
        r_ref.at[rows], *params, o_ref.at[rows], ybuf_ref, conv_ref, unperm_ref),
        tile_rows=C_TILES * SEQ_TILE)


def _ffn_tile(r_ref, p_ref, nrm_ref, wup_ref, dww_ref, dwb_ref, wdn_ref, plen_ref, wg_ref, wp_ref,
              fin_ref, o_ref, carry_ref, buf_ref, unperm_ref, *, final):
    ts = r_ref.shape[0]

    pe = _dot(p_ref[...].astype(_bf16), wp_ref[...])
    r = r_ref[...]
    h = _rms(r, nrm_ref[...]).astype(_bf16)
    n_slabs = FF_CHUNK // LANES

    def up(c):
        for half in range(2):
            col0 = half * D_FF + c * FF_CHUNK
            z = _dot(h, wup_ref[:, col0:col0 + FF_CHUNK])
            for s in range(n_slabs):
                slab = buf_ref.at[c % FF_SLOTS, half, s]
                lo = col0 + s * LANES
                slab[0:SUBLANES, :] = carry_ref[:, lo:lo + LANES]
                slab[SUBLANES:SUBLANES + ts, :] = z[:, s * LANES:(s + 1) * LANES]

    def gated(c):
        cols = []
        for s in range(n_slabs):
            convs = []
            for half in range(2):
                lo = half * D_FF + c * FF_CHUNK + s * LANES
                slab = buf_ref.at[c % FF_SLOTS, half, s]
                carry_ref[:, lo:lo + LANES] = slab[ts:ts + SUBLANES, :]
                convs.append(_conv3_phased(slab, ts, dww_ref[:, lo:lo + LANES])
                             + dwb_ref[:, lo:lo + LANES])
            g, u = convs
            cols.append(g * jax.nn.sigmoid(g) * u)
        return jnp.concatenate(cols, axis=1).astype(_bf16)

    for c in range(FF_SLOTS - 1):
        up(c)
    acc = None
    for c in range(N_FF_CHUNKS):
        if c + FF_SLOTS - 1 < N_FF_CHUNKS:
            up(c + FF_SLOTS - 1)
        d = _dot(gated(c), wdn_ref[c * FF_CHUNK:(c + 1) * FF_CHUNK, :])
        acc = d if acc is None else acc + d
    r2 = r + _natural_rows(acc, unperm_ref)

    gate = jax.nn.sigmoid(_dot(_rms(r2, plen_ref[...]).astype(_bf16), wg_ref[...]))
    r3 = r2 + gate * pe
    if final:
        r3 = _rms(r3, fin_ref[...])
    o_ref[...] = r3


def _ffn_kernel(r_ref, p_ref, *rest, final):
    params, o_ref, (carry_ref, buf_ref, unperm_ref) = rest[:-4], rest[-4], rest[-3:]

    @pl.when(pl.program_id(1) == 0)
    def _():
        carry_ref[...] = jnp.zeros_like(carry_ref)

    _for_each_tile(r_ref.shape[0], lambda rows: _ffn_tile(
        r_ref.at[rows], p_ref.at[rows], *params, o_ref.at[rows], carry_ref, buf_ref, unperm_ref,
        final=final))


def _rows(n):
    return pl.BlockSpec((None, SEQ_BLOCK, n), lambda b, j: (b, j, 0))


def _whole(a):
    nd = a.ndim
    return pl.BlockSpec(a.shape, lambda b, j: (0,) * nd, pipeline_mode=pl.Buffered(1))


def _call(body, name, batch, seq, row_inputs, params, scratch):
    grid = (batch, seq // SEQ_BLOCK)
    return pl.pallas_call(
        body,
        name=name,
        grid=grid,
        in_specs=[_rows(a.shape[-1]) for a in row_inputs] + [_whole(a) for a in params],
        out_specs=_rows(D_MODEL),
        out_shape=jax.ShapeDtypeStruct((batch, seq, D_MODEL), _f32),
        scratch_shapes=scratch,
        compiler_params=pltpu.CompilerParams(
            dimension_semantics=("arbitrary", "arbitrary"),
            vmem_limit_bytes=VMEM_LIMIT_BYTES),
    )(*row_inputs, *params)


def _row(v):
    return v.reshape(1, -1).astype(_f32)


def _cast_kernel(x_ref, o_ref):
    o_ref[...] = x_ref[...].astype(o_ref.dtype)


def _to_bf16(w, layer):
    _, rows, cols = w.shape
    block_rows = rows // CAST_STEPS
    return pl.pallas_call(
        _cast_kernel,
        name="to_bf16",
        grid=(CAST_STEPS,),
        in_specs=[pl.BlockSpec((None, block_rows, cols), lambda i: (layer, i, 0))],
        out_specs=pl.BlockSpec((block_rows, cols), lambda i: (i, 0)),
        out_shape=jax.ShapeDtypeStruct((rows, cols), _bf16),
    )(w)


def _even_layer(r, norm, w_in, a_ws, a_bs, a_ln_g, a_ln_b, b_conv_w, w_out):
    batch, seq, _ = r.shape
    params = [_row(norm), w_in, a_ws, a_bs.T, _row(a_ln_g), _row(a_ln_b), b_conv_w, w_out]
    scratch = [pltpu.VMEM((B_WIDTH // LANES, SUBLANES + SEQ_TILE, LANES), _f32),
               pltpu.VMEM((B_WIDTH // LANES, SEQ_TILE, LANES), _f32)]
    return _call(_even_kernel, "even_mixer", batch, seq, [r], params, scratch)


def _odd_layer(r, norm, w_in, b_in, dw_w, dw_b, ln_g, ln_b, w_out, b_out):
    batch, seq, _ = r.shape
    params = [_row(norm), w_in, _row(b_in), dw_w, _row(dw_b), _row(ln_g), _row(ln_b), w_out,
              _row(b_out)]
    scratch = [pltpu.VMEM((C_WIDTH // LANES, C_HALO + C_TILES * SEQ_TILE, LANES), _f32),
               pltpu.VMEM((C_TILES, SEQ_TILE, C_WIDTH), _f32),
               pltpu.VMEM((C_TILES, C_WIDTH // LANES, SEQ_TILE, LANES), _f32)]
    return _call(_odd_kernel, "odd_mixer", batch, seq, [r], params, scratch)


def _ffn_layer(r, p, norm, w_up, dw_w, dw_b, w_down, ple_norm, w_g, w_p, final_norm, final):
    batch, seq, _ = r.shape
    params = [_row(norm), w_up, dw_w, _row(dw_b), w_down, _row(ple_norm), w_g, w_p,
              _row(final_norm)]
    scratch = [pltpu.VMEM((SUBLANES, 2 * D_FF), _f32),
               pltpu.VMEM((FF_SLOTS, 2, FF_CHUNK // LANES, SUBLANES + SEQ_TILE, LANES), _f32),
               pltpu.VMEM((D_MODEL // LANES, SEQ_TILE, LANES), _f32)]
    return _call(functools.partial(_ffn_kernel, final=final), "ffn_final" if final else "ffn",
                 batch, seq, [r, p], params, scratch)


def kernel(x, p, ev_norm, ev_w_in, ev_a_ws, ev_a_bs, ev_a_ln_g, ev_a_ln_b, ev_b_conv_w, ev_w_out,
           od_norm, od_w_in, od_b_in, od_dw_w, od_dw_b, od_ln_g, od_ln_b, od_w_out, od_b_out,
           ffn_norm, ffn_w_up, ffn_dw_w, ffn_dw_b, ffn_w_down, ple_w_p, ple_norm, ple_w_g,
           final_norm):
    depth = p.shape[0]
    r = x
    for i in range(depth):
        j = i // 2
        if i % 2 == 0:
            r = _even_layer(r, ev_norm[j], _to_bf16(ev_w_in, j), ev_a_ws[j], ev_a_bs[j],
                            ev_a_ln_g[j], ev_a_ln_b[j], ev_b_conv_w[j], _to_bf16(ev_w_out, j))
        else:
            r = _odd_layer(r, od_norm[j], _to_bf16(od_w_in, j), od_b_in[j], od_dw_w[j], od_dw_b[j],
                           od_ln_g[j], od_ln_b[j], _to_bf16(od_w_out, j), od_b_out[j])
        r = _ffn_layer(r, p[i], ffn_norm[i], _to_bf16(ffn_w_up, i), ffn_dw_w[i], ffn_dw_b[i],
                       _to_bf16(ffn_w_down, i), ple_norm[i], _to_bf16(ple_w_g, i),
                       _to_bf16(ple_w_p, i), final_norm, final=(i == depth - 1))
    return r
```

```python
import functools
from typing import NamedTuple

import jax
import jax.numpy as jnp
from jax import lax
from jax.experimental import pallas as pl
from jax.experimental.pallas import tpu as pltpu

D_MODEL = 1024
PLE_DIM = 256
CHUNK = 128
A_HEADS = 4
A_HEAD_DIM = 128
A_WIDTH = A_HEADS * A_HEAD_DIM
B_WIDTH = 512
MIX_WIDTH = A_WIDTH + B_WIDTH
IN_WIDTH = 2 * A_WIDTH + 3 * B_WIDTH
C_WIDTH = D_MODEL
C_CONV = 31
D_FF = 2816
FFN_CONV = 3
EPS = 1e-6

SUBLANES = 8
LANES = 128
ROW_PHASES = 2
SEQ_TILE = 512
SEQ_BLOCK = 1024
FF_CHUNK = 256
N_FF_CHUNKS = D_FF // FF_CHUNK
FF_SLOTS = 4
C_HALO = 32
C_PHASES = 4
C_ROWS = 64
C_GROUP = 256
C_TILES = 1
CAST_STEPS = 8
VMEM_LIMIT_BYTES = 56 * 1024 * 1024

assert D_FF % FF_CHUNK == 0 and SEQ_TILE % CHUNK == 0 and C_HALO >= C_CONV - 1
assert SEQ_BLOCK % SEQ_TILE == 0

_bf16 = jnp.bfloat16
_f32 = jnp.float32


def _dot(a, b):
    return jnp.dot(a, b, preferred_element_type=_f32)


def _rms(x, g):
    ms = jnp.mean(x * x, axis=-1, keepdims=True)
    return x * lax.rsqrt(ms + EPS) * g


def _layernorm(x, g, b):
    mu = jnp.mean(x, axis=-1, keepdims=True)
    xc = x - mu
    var = jnp.mean(xc * xc, axis=-1, keepdims=True)
    return xc * lax.rsqrt(var + EPS) * g + b


def _for_each_tile(n_rows, tile, tile_rows=SEQ_TILE):
    def body(i, carry):
        tile(pl.ds(pl.multiple_of(i * tile_rows, tile_rows), tile_rows))
        return carry

    lax.fori_loop(0, n_rows // tile_rows, body, 0)


def _conv3_phased(slab_ref, t, w):
    n = t // ROW_PHASES
    base = SUBLANES - (FFN_CONV - 1)
    taps = [slab_ref[pl.ds(base + i, n, stride=ROW_PHASES), :]
            for i in range(ROW_PHASES + FFN_CONV - 1)]
    return jnp.concatenate(
        [sum(taps[ph + k] * w[k:k + 1, :] for k in range(FFN_CONV)) for ph in range(ROW_PHASES)],
        axis=0)


def _natural_rows(x, slabs_ref, phases=ROW_PHASES):
    t = x.shape[0]
    n = t // phases
    n_slabs = x.shape[1] // LANES
    for s in range(n_slabs):
        for ph in range(phases):
            slabs_ref[s, pl.ds(ph, n, stride=phases), :] = (
                x[ph * n:(ph + 1) * n, s * LANES:(s + 1) * LANES])
    return jnp.concatenate([slabs_ref[s] for s in range(n_slabs)], axis=1)


def _even_tile(r_ref, nrm_ref, win_ref, ws_ref, bs_ref, lng_ref, lnb_ref, cw_ref, wout_ref,
               o_ref, buf_ref, unperm_ref):
    ts = r_ref.shape[0]
    n_chunks = ts // CHUNK

    r = r_ref[...]
    h = _rms(r, nrm_ref[...]).astype(_bf16)
    z = _dot(h, win_ref[...])
    a_u = jax.nn.gelu(z[:, 0:A_WIDTH])
    a_v = jax.nn.gelu(z[:, A_WIDTH:2 * A_WIDTH])
    b_b = z[:, 2 * A_WIDTH:2 * A_WIDTH + B_WIDTH]
    b_c = z[:, 2 * A_WIDTH + B_WIDTH:2 * A_WIDTH + 2 * B_WIDTH]
    b_h = z[:, 2 * A_WIDTH + 2 * B_WIDTH:IN_WIDTH]

    row = lax.broadcasted_iota(jnp.int32, (CHUNK, CHUNK), 0)
    col = lax.broadcasted_iota(jnp.int32, (CHUNK, CHUNK), 1)
    tril = row >= col
    a_parts = []
    for hd in range(A_HEADS):
        lo = hd * A_HEAD_DIM
        vh = _layernorm(a_v[:, lo:lo + A_HEAD_DIM], lng_ref[:, lo:lo + A_HEAD_DIM],
                        lnb_ref[:, lo:lo + A_HEAD_DIM]).astype(_bf16)
        rhs = jnp.concatenate([vh[n * CHUNK:(n + 1) * CHUNK, :] for n in range(n_chunks)], axis=1)
        w = jnp.where(tril, ws_ref[hd], 0.0).astype(_bf16)
        mixed = _dot(w, rhs) + bs_ref[:, hd:hd + 1]
        mixed = jnp.concatenate(
            [mixed[:, n * A_HEAD_DIM:(n + 1) * A_HEAD_DIM] for n in range(n_chunks)], axis=0)
        a_parts.append(a_u[:, lo:lo + A_HEAD_DIM] * mixed)

    x = b_c * b_h
    convs = []
    for s in range(B_WIDTH // LANES):
        slab = buf_ref.at[s]
        slab[SUBLANES:SUBLANES + ts, :] = x[:, s * LANES:(s + 1) * LANES]
        convs.append(_conv3_phased(slab, ts, cw_ref[:, s * LANES:(s + 1) * LANES]))
        slab[0:SUBLANES, :] = slab[ts:ts + SUBLANES, :]
    b_out = b_b * _natural_rows(jnp.concatenate(convs, axis=1), unperm_ref)

    mix = jnp.concatenate(a_parts + [b_out], axis=1).astype(_bf16)
    o_ref[...] = r + _dot(mix, wout_ref[...])


def _even_kernel(r_ref, *rest):
    params, o_ref, (buf_ref, unperm_ref) = rest[:-3], rest[-3], rest[-2:]

    @pl.when(pl.program_id(1) == 0)
    def _():
        buf_ref[:, 0:SUBLANES, :] = jnp.zeros((B_WIDTH // LANES, SUBLANES, LANES), _f32)

    _for_each_tile(r_ref.shape[0], lambda rows: _even_tile(
        r_ref.at[rows], *params, o_ref.at[rows], buf_ref, unperm_ref))


def _odd_tile(r_ref, nrm_ref, win_ref, bin_ref, dww_ref, dwb_ref, lng_ref, lnb_ref, wout_ref,
              bout_ref, o_ref, ybuf_ref, conv_ref, unperm_ref):
    ts = SEQ_TILE
    n_slabs = C_WIDTH // LANES
    n_groups = C_WIDTH // C_GROUP
    n = ts // C_PHASES
    base = C_HALO - (C_CONV - 1)

    def glu(t, h, q):
        z_v, z_g = [_dot(h, win_ref[:, c0:c0 + C_GROUP]) + bin_ref[:, c0:c0 + C_GROUP]
                    for c0 in (q * C_GROUP, C_WIDTH + q * C_GROUP)]
        y = z_v * jax.nn.sigmoid(z_g)
        for j in range(C_GROUP // LANES):
            s = q * (C_GROUP // LANES) + j
            ybuf_ref[s, C_HALO + t * ts:C_HALO + (t + 1) * ts, :] = y[:, j * LANES:(j + 1) * LANES]

    def conv(t, s):
        slab = ybuf_ref.at[s]
        lanes = slice(s * LANES, (s + 1) * LANES)
        taps = [jnp.broadcast_to(dww_ref[k:k + 1, lanes], (C_ROWS, LANES)) for k in range(C_CONV)]
        bias = jnp.broadcast_to(dwb_ref[:, lanes], (C_ROWS, LANES))
        for mb in range(n // C_ROWS):
            accs = [bias] * C_PHASES
            for i in range(C_PHASES + C_CONV - 1):
                yi = slab[pl.ds(base + t * ts + i + C_PHASES * C_ROWS * mb, C_ROWS,
                                stride=C_PHASES), :]
                for ph in range(C_PHASES):
                    k = i - ph
                    if 0 <= k < C_CONV:
                        accs[ph] = accs[ph] + yi * taps[k]
            for ph in range(C_PHASES):
                conv_ref[t, ph * n + mb * C_ROWS:ph * n + (mb + 1) * C_ROWS, lanes] = accs[ph]

    def finish(t):
        y = _layernorm(conv_ref[t], lng_ref[...], lnb_ref[...])
        y = _natural_rows(y * jax.nn.sigmoid(y), unperm_ref.at[t], C_PHASES).astype(_bf16)
        o_ref[t * ts:(t + 1) * ts, :] = (r_ref[t * ts:(t + 1) * ts, :] + _dot(y, wout_ref[...])
                                         + bout_ref[...])

    for t in range(C_TILES):
        h = _rms(r_ref[t * ts:(t + 1) * ts, :], nrm_ref[...]).astype(_bf16)
        glu(t, h, 0)
        for q in range(n_groups):
            if q + 1 < n_groups:
                glu(t, h, q + 1)
            if q == 0 and t > 0:
                finish(t - 1)
            for j in range(C_GROUP // LANES):
                conv(t, q * (C_GROUP // LANES) + j)
    finish(C_TILES - 1)
    for s in range(n_slabs):
        ybuf_ref[s, 0:C_HALO, :] = ybuf_ref[s, C_TILES * ts:C_TILES * ts + C_HALO, :]


def _odd_kernel(r_ref, *rest):
    params, o_ref, (ybuf_ref, conv_ref, unperm_ref) = rest[:-4], rest[-4], rest[-3:]

    @pl.when(pl.program_id(1) == 0)
    def _():
        ybuf_ref[:, 0:C_HALO, :] = jnp.zeros((C_WIDTH // LANES, C_HALO, LANES), _f32)

    _for_each_tile(r_ref.shape[0], lambda rows: _odd_tile(
        r_ref.at[rows], *params, o_ref.at[rows], ybuf_ref, conv_ref, unperm_ref),
        tile_rows=C_TILES * SEQ_TILE)


def _ffn_tile(r_ref, p_ref, nrm_ref, wup_ref, dww_ref, dwb_ref, wdn_ref, plen_ref, wg_ref, wp_ref,
              fin_ref, o_ref, carry_ref, buf_ref, unperm_ref, *, final):
    ts = r_ref.shape[0]

    pe = _dot(p_ref[...].astype(_bf16), wp_ref[...])
    r = r_ref[...]
    h = _rms(r, nrm_ref[...]).astype(_bf16)
    n_slabs = FF_CHUNK // LANES

    def up(c):
        for half in range(2):
            col0 = half * D_FF + c * FF_CHUNK
            z = _dot(h, wup_ref[:, col0:col0 + FF_CHUNK])
            for s in range(n_slabs):
                slab = buf_ref.at[c % FF_SLOTS, half, s]
                lo = col0 + s * LANES
                slab[0:SUBLANES, :] = carry_ref[:, lo:lo + LANES]
                slab[SUBLANES:SUBLANES + ts, :] = z[:, s * LANES:(s + 1) * LANES]

    def gated(c):
        cols = []
        for s in range(n_slabs):
            convs = []
            for half in range(2):
                lo = half * D_FF + c * FF_CHUNK + s * LANES
                slab = buf_ref.at[c % FF_SLOTS, half, s]
                carry_ref[:, lo:lo + LANES] = slab[ts:ts + SUBLANES, :]
                convs.append(_conv3_phased(slab, ts, dww_ref[:, lo:lo + LANES])
                             + dwb_ref[:, lo:lo + LANES])
            g, u = convs
            cols.append(g * jax.nn.sigmoid(g) * u)
        return jnp.concatenate(cols, axis=1).astype(_bf16)

    for c in range(FF_SLOTS - 1):
        up(c)
    acc = None
    for c in range(N_FF_CHUNKS):
        if c + FF_SLOTS - 1 < N_FF_CHUNKS:
            up(c + FF_SLOTS - 1)
        d = _dot(gated(c), wdn_ref[c * FF_CHUNK:(c + 1) * FF_CHUNK, :])
        acc = d if acc is None else acc + d
    r2 = r + _natural_rows(acc, unperm_ref)

    gate = jax.nn.sigmoid(_dot(_rms(r2, plen_ref[...]).astype(_bf16), wg_ref[...]))
    r3 = r2 + gate * pe
    if final:
        r3 = _rms(r3, fin_ref[...])
    o_ref[...] = r3


def _ffn_kernel(r_ref, p_ref, *rest, final):
    params, o_ref, (carry_ref, buf_ref, unperm_ref) = rest[:-4], rest[-4], rest[-3:]

    @pl.when(pl.program_id(1) == 0)
    def _():
        carry_ref[...] = jnp.zeros_like(carry_ref)

    _for_each_tile(r_ref.shape[0], lambda rows: _ffn_tile(
        r_ref.at[rows], p_ref.at[rows], *params, o_ref.at[rows], carry_ref, buf_ref, unperm_ref,
        final=final))


def _rows(n, layer=None):
    if layer is None:
        return pl.BlockSpec((None, SEQ_BLOCK, n), lambda b, j: (b, j, 0))
    return pl.BlockSpec((None, None, SEQ_BLOCK, n), lambda b, j: (layer, b, j, 0))


class _Layer(NamedTuple):
    stacked: jax.Array
    layer: int


def _whole(a):
    if isinstance(a, _Layer):
        zeros = (0,) * (a.stacked.ndim - 1)
        return pl.BlockSpec((None,) + a.stacked.shape[1:], lambda b, j: (a.layer,) + zeros,
                            pipeline_mode=pl.Buffered(1))
    nd = a.ndim
    return pl.BlockSpec(a.shape, lambda b, j: (0,) * nd, pipeline_mode=pl.Buffered(1))


def _call(body, name, batch, seq, row_inputs, params, scratch, row_layers=None):
    grid = (batch, seq // SEQ_BLOCK)
    row_layers = row_layers or [None] * len(row_inputs)
    return pl.pallas_call(
        body,
        name=name,
        grid=grid,
        in_specs=([_rows(a.shape[-1], layer) for a, layer in zip(row_inputs, row_layers)]
                  + [_whole(a) for a in params]),
        out_specs=_rows(D_MODEL),
        out_shape=jax.ShapeDtypeStruct((batch, seq, D_MODEL), _f32),
        scratch_shapes=scratch,
        compiler_params=pltpu.CompilerParams(
            dimension_semantics=("arbitrary", "arbitrary"),
            vmem_limit_bytes=VMEM_LIMIT_BYTES),
    )(*row_inputs, *[a.stacked if isinstance(a, _Layer) else a for a in params])


def _row(v):
    return v.reshape(1, -1).astype(_f32)


def _cast_kernel(x_ref, o_ref):
    o_ref[...] = x_ref[...].astype(o_ref.dtype)


def _to_bf16(w):
    layers, rows, cols = w.shape
    block_rows = rows // CAST_STEPS
    spec = pl.BlockSpec((None, block_rows, cols), lambda l, i: (l, i, 0))
    return pl.pallas_call(
        _cast_kernel,
        name="to_bf16",
        grid=(layers, CAST_STEPS),
        in_specs=[spec],
        out_specs=spec,
        out_shape=jax.ShapeDtypeStruct(w.shape, _bf16),
    )(w)


def _even_layer(r, norm, w_in, a_ws, a_bs, a_ln_g, a_ln_b, b_conv_w, w_out):
    batch, seq, _ = r.shape
    params = [_row(norm), w_in, a_ws, a_bs.T, _row(a_ln_g), _row(a_ln_b), b_conv_w, w_out]
    scratch = [pltpu.VMEM((B_WIDTH // LANES, SUBLANES + SEQ_TILE, LANES), _f32),
               pltpu.VMEM((B_WIDTH // LANES, SEQ_TILE, LANES), _f32)]
    return _call(_even_kernel, "even_mixer", batch, seq, [r], params, scratch)


def _odd_layer(r, norm, w_in, b_in, dw_w, dw_b, ln_g, ln_b, w_out, b_out):
    batch, seq, _ = r.shape
    params = [_row(norm), w_in, _row(b_in), dw_w, _row(dw_b), _row(ln_g), _row(ln_b), w_out,
              _row(b_out)]
    scratch = [pltpu.VMEM((C_WIDTH // LANES, C_HALO + C_TILES * SEQ_TILE, LANES), _f32),
               pltpu.VMEM((C_TILES, SEQ_TILE, C_WIDTH), _f32),
               pltpu.VMEM((C_TILES, C_WIDTH // LANES, SEQ_TILE, LANES), _f32)]
    return _call(_odd_kernel, "odd_mixer", batch, seq, [r], params, scratch)


def _ffn_layer(r, p, layer, norm, w_up, dw_w, dw_b, w_down, ple_norm, w_g, w_p, final_norm, final):
    batch, seq, _ = r.shape
    params = [_row(norm), w_up, dw_w, _row(dw_b), w_down, _row(ple_norm), w_g, w_p,
              _row(final_norm)]
    scratch = [pltpu.VMEM((SUBLANES, 2 * D_FF), _f32),
               pltpu.VMEM((FF_SLOTS, 2, FF_CHUNK // LANES, SUBLANES + SEQ_TILE, LANES), _f32),
               pltpu.VMEM((D_MODEL // LANES, SEQ_TILE, LANES), _f32)]
    return _call(functools.partial(_ffn_kernel, final=final), "ffn_final" if final else "ffn",
                 batch, seq, [r, p], params, scratch, row_layers=[None, layer])


def kernel(x, p, ev_norm, ev_w_in, ev_a_ws, ev_a_bs, ev_a_ln_g, ev_a_ln_b, ev_b_conv_w, ev_w_out,
           od_norm, od_w_in, od_b_in, od_dw_w, od_dw_b, od_ln_g, od_ln_b, od_w_out, od_b_out,
           ffn_norm, ffn_w_up, ffn_dw_w, ffn_dw_b, ffn_w_down, ple_w_p, ple_norm, ple_w_g,
           final_norm):
    depth = p.shape[0]
    ev_w_in, ev_w_out, od_w_in, od_w_out, ffn_w_up, ffn_w_down, ple_w_g, ple_w_p = [
        _to_bf16(w) for w in (ev_w_in, ev_w_out, od_w_in, od_w_out, ffn_w_up, ffn_w_down, ple_w_g,
                              ple_w_p)]
    r = x
    for i in range(depth):
        j = i // 2
        if i % 2 == 0:
            r = _even_layer(r, ev_norm[j], _Layer(ev_w_in, j), ev_a_ws[j], ev_a_bs[j],
                            ev_a_ln_g[j], ev_a_ln_b[j], ev_b_conv_w[j], _Layer(ev_w_out, j))
        else:
            r = _odd_layer(r, od_norm[j], _Layer(od_w_in, j), od_b_in[j], od_dw_w[j], od_dw_b[j],
                           od_ln_g[j], od_ln_b[j], _Layer(od_w_out, j), od_b_out[j])
        r = _ffn_layer(r, p, i, ffn_norm[i], _Layer(ffn_w_up, i), ffn_dw_w[i], ffn_dw_b[i],
                       _Layer(ffn_w_down, i), ple_norm[i], _Layer(ple_w_g, i),
                       _Layer(ple_w_p, i), final_norm, final=(i == depth - 1))
    return r
```

```python
import functools
from typing import NamedTuple

import jax
import jax.numpy as jnp
from jax import lax
from jax.experimental import pallas as pl
from jax.experimental.pallas import tpu as pltpu

D_MODEL = 1024
PLE_DIM = 256
CHUNK = 128
A_HEADS = 4
A_HEAD_DIM = 128
A_WIDTH = A_HEADS * A_HEAD_DIM
B_WIDTH = 512
MIX_WIDTH = A_WIDTH + B_WIDTH
IN_WIDTH = 2 * A_WIDTH + 3 * B_WIDTH
C_WIDTH = D_MODEL
C_CONV = 31
D_FF = 2816
FFN_CONV = 3
EPS = 1e-6

SUBLANES = 8
LANES = 128
ROW_PHASES = 2
SEQ_TILE = 512
SEQ_BLOCK = 1024
FF_CHUNK = 256
N_FF_CHUNKS = D_FF // FF_CHUNK
FF_SLOTS = 4
C_HALO = 32
C_PHASES = 2
C_ROWS = 128
C_GROUP = 256
C_TILES = 1
CAST_STEPS = 16
CAST_VMEM_LIMIT_BYTES = 40 * 1024 * 1024
VMEM_LIMIT_BYTES = 56 * 1024 * 1024

assert D_FF % FF_CHUNK == 0 and SEQ_TILE % CHUNK == 0 and C_HALO >= C_CONV - 1
assert SEQ_BLOCK % SEQ_TILE == 0

_bf16 = jnp.bfloat16
_f32 = jnp.float32


def _dot(a, b):
    return jnp.dot(a, b, preferred_element_type=_f32)


def _rms(x, g):
    ms = jnp.mean(x * x, axis=-1, keepdims=True)
    return x * lax.rsqrt(ms + EPS) * g


def _layernorm(x, g, b):
    mu = jnp.mean(x, axis=-1, keepdims=True)
    xc = x - mu
    var = jnp.mean(xc * xc, axis=-1, keepdims=True)
    return xc * lax.rsqrt(var + EPS) * g + b


def _for_each_tile(n_rows, tile, tile_rows=SEQ_TILE):
    def body(i, carry):
        tile(pl.ds(pl.multiple_of(i * tile_rows, tile_rows), tile_rows))
        return carry

    lax.fori_loop(0, n_rows // tile_rows, body, 0)


def _conv3_phased(slab_ref, t, w):
    n = t // ROW_PHASES
    base = SUBLANES - (FFN_CONV - 1)
    taps = [slab_ref[pl.ds(base + i, n, stride=ROW_PHASES), :]
            for i in range(ROW_PHASES + FFN_CONV - 1)]
    return jnp.concatenate(
        [sum(taps[ph + k] * w[k:k + 1, :] for k in range(FFN_CONV)) for ph in range(ROW_PHASES)],
        axis=0)


def _natural_rows(x, slabs_ref, phases=ROW_PHASES):
    t = x.shape[0]
    n = t // phases
    n_slabs = x.shape[1] // LANES
    for s in range(n_slabs):
        for ph in range(phases):
            slabs_ref[s, pl.ds(ph, n, stride=phases), :] = (
                x[ph * n:(ph + 1) * n, s * LANES:(s + 1) * LANES])
    return jnp.concatenate([slabs_ref[s] for s in range(n_slabs)], axis=1)


def _even_tile(r_ref, nrm_ref, win_ref, ws_ref, bs_ref, lng_ref, lnb_ref, cw_ref, wout_ref,
               o_ref, buf_ref, unperm_ref):
    ts = r_ref.shape[0]
    n_chunks = ts // CHUNK

    r = r_ref[...]
    h = _rms(r, nrm_ref[...]).astype(_bf16)
    z = _dot(h, win_ref[...])
    a_u = jax.nn.gelu(z[:, 0:A_WIDTH])
    a_v = jax.nn.gelu(z[:, A_WIDTH:2 * A_WIDTH])
    b_b = z[:, 2 * A_WIDTH:2 * A_WIDTH + B_WIDTH]
    b_c = z[:, 2 * A_WIDTH + B_WIDTH:2 * A_WIDTH + 2 * B_WIDTH]
    b_h = z[:, 2 * A_WIDTH + 2 * B_WIDTH:IN_WIDTH]

    row = lax.broadcasted_iota(jnp.int32, (CHUNK, CHUNK), 0)
    col = lax.broadcasted_iota(jnp.int32, (CHUNK, CHUNK), 1)
    tril = row >= col
    a_parts = []
    for hd in range(A_HEADS):
        lo = hd * A_HEAD_DIM
        vh = _layernorm(a_v[:, lo:lo + A_HEAD_DIM], lng_ref[:, lo:lo + A_HEAD_DIM],
                        lnb_ref[:, lo:lo + A_HEAD_DIM]).astype(_bf16)
        rhs = jnp.concatenate([vh[n * CHUNK:(n + 1) * CHUNK, :] for n in range(n_chunks)], axis=1)
        w = jnp.where(tril, ws_ref[hd], 0.0).astype(_bf16)
        mixed = _dot(w, rhs) + bs_ref[:, hd:hd + 1]
        mixed = jnp.concatenate(
            [mixed[:, n * A_HEAD_DIM:(n + 1) * A_HEAD_DIM] for n in range(n_chunks)], axis=0)
        a_parts.append(a_u[:, lo:lo + A_HEAD_DIM] * mixed)

    x = b_c * b_h
    convs = []
    for s in range(B_WIDTH // LANES):
        slab = buf_ref.at[s]
        slab[SUBLANES:SUBLANES + ts, :] = x[:, s * LANES:(s + 1) * LANES]
        convs.append(_conv3_phased(slab, ts, cw_ref[:, s * LANES:(s + 1) * LANES]))
        slab[0:SUBLANES, :] = slab[ts:ts + SUBLANES, :]
    b_out = b_b * _natural_rows(jnp.concatenate(convs, axis=1), unperm_ref)

    mix = jnp.concatenate(a_parts + [b_out], axis=1).astype(_bf16)
    o_ref[...] = r + _dot(mix, wout_ref[...])


def _even_kernel(r_ref, *rest):
    params, o_ref, (buf_ref, unperm_ref) = rest[:-3], rest[-3], rest[-2:]

    @pl.when(pl.program_id(1) == 0)
    def _():
        buf_ref[:, 0:SUBLANES, :] = jnp.zeros((B_WIDTH // LANES, SUBLANES, LANES), _f32)

    _for_each_tile(r_ref.shape[0], lambda rows: _even_tile(
        r_ref.at[rows], *params, o_ref.at[rows], buf_ref, unperm_ref))


def _odd_tile(r_ref, nrm_ref, win_ref, bin_ref, dww_ref, dwb_ref, lng_ref, lnb_ref, wout_ref,
              bout_ref, o_ref, ybuf_ref, conv_ref, unperm_ref):
    ts = SEQ_TILE
    n_slabs = C_WIDTH // LANES
    n_groups = C_WIDTH // C_GROUP
    n = ts // C_PHASES
    base = C_HALO - (C_CONV - 1)

    def glu(t, h, q):
        z_v, z_g = [_dot(h, win_ref[:, c0:c0 + C_GROUP]) + bin_ref[:, c0:c0 + C_GROUP]
                    for c0 in (q * C_GROUP, C_WIDTH + q * C_GROUP)]
        y = z_v * jax.nn.sigmoid(z_g)
        for j in range(C_GROUP // LANES):
            s = q * (C_GROUP // LANES) + j
            ybuf_ref[s, C_HALO + t * ts:C_HALO + (t + 1) * ts, :] = y[:, j * LANES:(j + 1) * LANES]

    def conv(t, s):
        slab = ybuf_ref.at[s]
        lanes = slice(s * LANES, (s + 1) * LANES)
        taps = [jnp.broadcast_to(dww_ref[k:k + 1, lanes], (C_ROWS, LANES)) for k in range(C_CONV)]
        bias = jnp.broadcast_to(dwb_ref[:, lanes], (C_ROWS, LANES))
        for mb in range(n // C_ROWS):
            accs = [bias] * C_PHASES
            for i in range(C_PHASES + C_CONV - 1):
                yi = slab[pl.ds(base + t * ts + i + C_PHASES * C_ROWS * mb, C_ROWS,
                                stride=C_PHASES), :]
                for ph in range(C_PHASES):
                    k = i - ph
                    if 0 <= k < C_CONV:
                        accs[ph] = accs[ph] + yi * taps[k]
            for ph in range(C_PHASES):
                conv_ref[t, ph * n + mb * C_ROWS:ph * n + (mb + 1) * C_ROWS, lanes] = accs[ph]

    def finish(t):
        y = _layernorm(conv_ref[t], lng_ref[...], lnb_ref[...])
        y = _natural_rows(y * jax.nn.sigmoid(y), unperm_ref.at[t], C_PHASES).astype(_bf16)
        o_ref[t * ts:(t + 1) * ts, :] = (r_ref[t * ts:(t + 1) * ts, :] + _dot(y, wout_ref[...])
                                         + bout_ref[...])

    for t in range(C_TILES):
        h = _rms(r_ref[t * ts:(t + 1) * ts, :], nrm_ref[...]).astype(_bf16)
        glu(t, h, 0)
        for q in range(n_groups):
            if q + 1 < n_groups:
                glu(t, h, q + 1)
            if q == 0 and t > 0:
                finish(t - 1)
            for j in range(C_GROUP // LANES):
                conv(t, q * (C_GROUP // LANES) + j)
    finish(C_TILES - 1)
    for s in range(n_slabs):
        ybuf_ref[s, 0:C_HALO, :] = ybuf_ref[s, C_TILES * ts:C_TILES * ts + C_HALO, :]


def _odd_kernel(r_ref, *rest):
    params, o_ref, (ybuf_ref, conv_ref, unperm_ref) = rest[:-4], rest[-4], rest[-3:]

    @pl.when(pl.program_id(1) == 0)
    def _():
        ybuf_ref[:, 0:C_HALO, :] = jnp.zeros((C_WIDTH // LANES, C_HALO, LANES), _f32)

    _for_each_tile(r_ref.shape[0], lambda rows: _odd_tile(
        r_ref.at[rows], *params, o_ref.at[rows], ybuf_ref, conv_ref, unperm_ref),
        tile_rows=C_TILES * SEQ_TILE)


def _ffn_tile(r_ref, p_ref, nrm_ref, wup_ref, dww_ref, dwb_ref, wdn_ref, plen_ref, wg_ref, wp_ref,
              fin_ref, o_ref, carry_ref, buf_ref, unperm_ref, *, final):
    ts = r_ref.shape[0]

    pe = _dot(p_ref[...].astype(_bf16), wp_ref[...])
    r = r_ref[...]
    h = _rms(r, nrm_ref[...]).astype(_bf16)
    n_slabs = FF_CHUNK // LANES

    def up(c):
        for half in range(2):
            col0 = half * D_FF + c * FF_CHUNK
            z = _dot(h, wup_ref[:, col0:col0 + FF_CHUNK])
            for s in range(n_slabs):
                slab = buf_ref.at[c % FF_SLOTS, half, s]
                lo = col0 + s * LANES
                slab[0:SUBLANES, :] = carry_ref[:, lo:lo + LANES]
                slab[SUBLANES:SUBLANES + ts, :] = z[:, s * LANES:(s + 1) * LANES]

    def gated(c):
        cols = []
        for s in range(n_slabs):
            convs = []
            for half in range(2):
                lo = half * D_FF + c * FF_CHUNK + s * LANES
                slab = buf_ref.at[c % FF_SLOTS, half, s]
                carry_ref[:, lo:lo + LANES] = slab[ts:ts + SUBLANES, :]
                convs.append(_conv3_phased(slab, ts, dww_ref[:, lo:lo + LANES])
                             + dwb_ref[:, lo:lo + LANES])
            g, u = convs
            cols.append(g * jax.nn.sigmoid(g) * u)
        return jnp.concatenate(cols, axis=1).astype(_bf16)

    for c in range(FF_SLOTS - 1):
        up(c)
    acc = None
    for c in range(N_FF_CHUNKS):
        if c + FF_SLOTS - 1 < N_FF_CHUNKS:
            up(c + FF_SLOTS - 1)
        d = _dot(gated(c), wdn_ref[c * FF_CHUNK:(c + 1) * FF_CHUNK, :])
        acc = d if acc is None else acc + d
    r2 = r + _natural_rows(acc, unperm_ref)

    gate = jax.nn.sigmoid(_dot(_rms(r2, plen_ref[...]).astype(_bf16), wg_ref[...]))
    r3 = r2 + gate * pe
    if final:
        r3 = _rms(r3, fin_ref[...])
    o_ref[...] = r3


def _ffn_kernel(r_ref, p_ref, *rest, final):
    params, o_ref, (carry_ref, buf_ref, unperm_ref) = rest[:-4], rest[-4], rest[-3:]

    @pl.when(pl.program_id(1) == 0)
    def _():
        carry_ref[...] = jnp.zeros_like(carry_ref)

    _for_each_tile(r_ref.shape[0], lambda rows: _ffn_tile(
        r_ref.at[rows], p_ref.at[rows], *params, o_ref.at[rows], carry_ref, buf_ref, unperm_ref,
        final=final))


def _rows(n, layer=None):
    if layer is None:
        return pl.BlockSpec((None, SEQ_BLOCK, n), lambda b, j: (b, j, 0))
    return pl.BlockSpec((None, None, SEQ_BLOCK, n), lambda b, j: (layer, b, j, 0))


class _Layer(NamedTuple):
    stacked: jax.Array
    layer: int


def _whole(a):
    if isinstance(a, _Layer):
        zeros = (0,) * (a.stacked.ndim - 1)
        return pl.BlockSpec((None,) + a.stacked.shape[1:], lambda b, j: (a.layer,) + zeros,
                            pipeline_mode=pl.Buffered(1))
    nd = a.ndim
    return pl.BlockSpec(a.shape, lambda b, j: (0,) * nd, pipeline_mode=pl.Buffered(1))


def _call(body, name, batch, seq, row_inputs, params, scratch, row_layers=None):
    grid = (batch, seq // SEQ_BLOCK)
    row_layers = row_layers or [None] * len(row_inputs)
    return pl.pallas_call(
        body,
        name=name,
        grid=grid,
        in_specs=([_rows(a.shape[-1], layer) for a, layer in zip(row_inputs, row_layers)]
                  + [_whole(a) for a in params]),
        out_specs=_rows(D_MODEL),
        out_shape=jax.ShapeDtypeStruct((batch, seq, D_MODEL), _f32),
        scratch_shapes=scratch,
        compiler_params=pltpu.CompilerParams(
            dimension_semantics=("arbitrary", "arbitrary"),
            vmem_limit_bytes=VMEM_LIMIT_BYTES),
    )(*row_inputs, *[a.stacked if isinstance(a, _Layer) else a for a in params])


def _row(v):
    return v.reshape(1, -1).astype(_f32)


def _cast_kernel(*refs):
    n = len(refs) // 2
    for x_ref, o_ref in zip(refs[:n], refs[n:]):
        o_ref[...] = x_ref[...].astype(o_ref.dtype)


def _to_bf16(ws):
    def spec(w):
        layers, rows, cols = w.shape
        return pl.BlockSpec((layers, rows // CAST_STEPS, cols), lambda i: (0, i, 0))

    return pl.pallas_call(
        _cast_kernel,
        name="to_bf16",
        grid=(CAST_STEPS,),
        in_specs=[spec(w) for w in ws],
        out_specs=[spec(w) for w in ws],
        out_shape=[jax.ShapeDtypeStruct(w.shape, _bf16) for w in ws],
        compiler_params=pltpu.CompilerParams(vmem_limit_bytes=CAST_VMEM_LIMIT_BYTES),
    )(*ws)


def _even_layer(r, norm, w_in, a_ws, a_bs, a_ln_g, a_ln_b, b_conv_w, w_out):
    batch, seq, _ = r.shape
    params = [_row(norm), w_in, a_ws, a_bs.T, _row(a_ln_g), _row(a_ln_b), b_conv_w, w_out]
    scratch = [pltpu.VMEM((B_WIDTH // LANES, SUBLANES + SEQ_TILE, LANES), _f32),
               pltpu.VMEM((B_WIDTH // LANES, SEQ_TILE, LANES), _f32)]
    return _call(_even_kernel, "even_mixer", batch, seq, [r], params, scratch)


def _odd_layer(r, norm, w_in, b_in, dw_w, dw_b, ln_g, ln_b, w_out, b_out):
    batch, seq, _ = r.shape
    params = [_row(norm), w_in, _row(b_in), dw_w, _row(dw_b), _row(ln_g), _row(ln_b), w_out,
              _row(b_out)]
    scratch = [pltpu.VMEM((C_WIDTH // LANES, C_HALO + C_TILES * SEQ_TILE, LANES), _f32),
               pltpu.VMEM((C_TILES, SEQ_TILE, C_WIDTH), _f32),
               pltpu.VMEM((C_TILES, C_WIDTH // LANES, SEQ_TILE, LANES), _f32)]
    return _call(_odd_kernel, "odd_mixer", batch, seq, [r], params, scratch)


def _ffn_layer(r, p, layer, norm, w_up, dw_w, dw_b, w_down, ple_norm, w_g, w_p, final_norm, final):
    batch, seq, _ = r.shape
    params = [_row(norm), w_up, dw_w, _row(dw_b), w_down, _row(ple_norm), w_g, w_p,
              _row(final_norm)]
    scratch = [pltpu.VMEM((SUBLANES, 2 * D_FF), _f32),
               pltpu.VMEM((FF_SLOTS, 2, FF_CHUNK // LANES, SUBLANES + SEQ_TILE, LANES), _f32),
               pltpu.VMEM((D_MODEL // LANES, SEQ_TILE, LANES), _f32)]
    return _call(functools.partial(_ffn_kernel, final=final), "ffn_final" if final else "ffn",
                 batch, seq, [r, p], params, scratch, row_layers=[None, layer])


def kernel(x, p, ev_norm, ev_w_in, ev_a_ws, ev_a_bs, ev_a_ln_g, ev_a_ln_b, ev_b_conv_w, ev_w_out,
           od_norm, od_w_in, od_b_in, od_dw_w, od_dw_b, od_ln_g, od_ln_b, od_w_out, od_b_out,
           ffn_norm, ffn_w_up, ffn_dw_w, ffn_dw_b, ffn_w_down, ple_w_p, ple_norm, ple_w_g,
           final_norm):
    depth = p.shape[0]
    ev_w_in, ev_w_out, od_w_in, od_w_out, ffn_w_up, ffn_w_down, ple_w_g, ple_w_p = _to_bf16(
        [ev_w_in, ev_w_out, od_w_in, od_w_out, ffn_w_up, ffn_w_down, ple_w_g, ple_w_p])
    r = x
    for i in range(depth):
        j = i // 2
        if i % 2 == 0:
            r = _even_layer(r, ev_norm[j], _Layer(ev_w_in, j), ev_a_ws[j], ev_a_bs[j],
                            ev_a_ln_g[j], ev_a_ln_b[j], ev_b_conv_w[j], _Layer(ev_w_out, j))
        else:
            r = _odd_layer(r, od_norm[j], _Layer(od_w_in, j), od_b_in[j], od_dw_w[j], od_dw_b[j],
                           od_ln_g[j], od_ln_b[j], _Layer(od_w_out, j), od_b_out[j])
        r = _ffn_layer(r, p, i, ffn_norm[i], _Layer(ffn_w_up, i), ffn_dw_w[i], ffn_dw_b[i],
                       _Layer(ffn_w_down, i), ple_norm[i], _Layer(ple_w_g, i),
                       _Layer(ple_w_p, i), final_norm, final=(i == depth - 1))
    return r
```

```python
import functools
from typing import NamedTuple

import jax
import jax.numpy as jnp
from jax import lax
from jax.experimental import pallas as pl
from jax.experimental.pallas import tpu as pltpu

D_MODEL = 1024
PLE_DIM = 256
CHUNK = 128
A_HEADS = 4
A_HEAD_DIM = 128
A_WIDTH = A_HEADS * A_HEAD_DIM
B_WIDTH = 512
MIX_WIDTH = A_WIDTH + B_WIDTH
IN_WIDTH = 2 * A_WIDTH + 3 * B_WIDTH
C_WIDTH = D_MODEL
C_CONV = 31
D_FF = 2816
FFN_CONV = 3
EPS = 1e-6

SUBLANES = 8
LANES = 128
ROW_PHASES = 2
SEQ_TILE = 512
SEQ_BLOCK = 1024
FF_CHUNK = 256
N_FF_CHUNKS = D_FF // FF_CHUNK
FF_SLOTS = 4
C_HALO = 32
C_PHASES = 2
C_ROWS = 128
C_GROUP = 256
C_TILES = 1
CAST_STEPS = 16
CAST_VMEM_LIMIT_BYTES = 40 * 1024 * 1024
VMEM_LIMIT_BYTES = 56 * 1024 * 1024

assert D_FF % FF_CHUNK == 0 and SEQ_TILE % CHUNK == 0 and C_HALO >= C_CONV - 1
assert SEQ_BLOCK % SEQ_TILE == 0

_bf16 = jnp.bfloat16
_f32 = jnp.float32


def _dot(a, b):
    return jnp.dot(a, b, preferred_element_type=_f32)


def _rms(x, g):
    ms = jnp.mean(x * x, axis=-1, keepdims=True)
    return x * lax.rsqrt(ms + EPS) * g


def _layernorm(x, g, b):
    mu = jnp.mean(x, axis=-1, keepdims=True)
    xc = x - mu
    var = jnp.mean(xc * xc, axis=-1, keepdims=True)
    return xc * lax.rsqrt(var + EPS) * g + b


def _for_each_tile(n_rows, tile, tile_rows=SEQ_TILE):
    def body(i, carry):
        tile(pl.ds(pl.multiple_of(i * tile_rows, tile_rows), tile_rows))
        return carry

    lax.fori_loop(0, n_rows // tile_rows, body, 0)


def _conv3_phased(slab_ref, t, w):
    n = t // ROW_PHASES
    base = SUBLANES - (FFN_CONV - 1)
    taps = [slab_ref[pl.ds(base + i, n, stride=ROW_PHASES), :]
            for i in range(ROW_PHASES + FFN_CONV - 1)]
    return jnp.concatenate(
        [sum(taps[ph + k] * w[k:k + 1, :] for k in range(FFN_CONV)) for ph in range(ROW_PHASES)],
        axis=0)


def _natural_rows(x, slabs_ref, phases=ROW_PHASES):
    t = x.shape[0]
    n = t // phases
    n_slabs = x.shape[1] // LANES
    for s in range(n_slabs):
        for ph in range(phases):
            slabs_ref[s, pl.ds(ph, n, stride=phases), :] = (
                x[ph * n:(ph + 1) * n, s * LANES:(s + 1) * LANES])
    return jnp.concatenate([slabs_ref[s] for s in range(n_slabs)], axis=1)


def _even_tile(r_ref, nrm_ref, win_ref, ws_ref, bs_ref, lng_ref, lnb_ref, cw_ref, wout_ref,
               o_ref, buf_ref, unperm_ref):
    ts = r_ref.shape[0]
    n_chunks = ts // CHUNK

    r = r_ref[...]
    h = _rms(r, nrm_ref[...]).astype(_bf16)
    z = _dot(h, win_ref[...])
    a_u = jax.nn.gelu(z[:, 0:A_WIDTH])
    a_v = jax.nn.gelu(z[:, A_WIDTH:2 * A_WIDTH])
    b_b = z[:, 2 * A_WIDTH:2 * A_WIDTH + B_WIDTH]
    b_c = z[:, 2 * A_WIDTH + B_WIDTH:2 * A_WIDTH + 2 * B_WIDTH]
    b_h = z[:, 2 * A_WIDTH + 2 * B_WIDTH:IN_WIDTH]

    row = lax.broadcasted_iota(jnp.int32, (CHUNK, CHUNK), 0)
    col = lax.broadcasted_iota(jnp.int32, (CHUNK, CHUNK), 1)
    tril = row >= col
    a_parts = []
    for hd in range(A_HEADS):
        lo = hd * A_HEAD_DIM
        vh = _layernorm(a_v[:, lo:lo + A_HEAD_DIM], lng_ref[:, lo:lo + A_HEAD_DIM],
                        lnb_ref[:, lo:lo + A_HEAD_DIM]).astype(_bf16)
        rhs = jnp.concatenate([vh[n * CHUNK:(n + 1) * CHUNK, :] for n in range(n_chunks)], axis=1)
        w = jnp.where(tril, ws_ref[hd], 0.0).astype(_bf16)
        mixed = _dot(w, rhs) + bs_ref[:, hd:hd + 1]
        mixed = jnp.concatenate(
            [mixed[:, n * A_HEAD_DIM:(n + 1) * A_HEAD_DIM] for n in range(n_chunks)], axis=0)
        a_parts.append(a_u[:, lo:lo + A_HEAD_DIM] * mixed)

    x = b_c * b_h
    convs = []
    for s in range(B_WIDTH // LANES):
        slab = buf_ref.at[s]
        slab[SUBLANES:SUBLANES + ts, :] = x[:, s * LANES:(s + 1) * LANES]
        convs.append(_conv3_phased(slab, ts, cw_ref[:, s * LANES:(s + 1) * LANES]))
        slab[0:SUBLANES, :] = slab[ts:ts + SUBLANES, :]
    b_out = b_b * _natural_rows(jnp.concatenate(convs, axis=1), unperm_ref)

    mix = jnp.concatenate(a_parts + [b_out], axis=1).astype(_bf16)
    o_ref[...] = r + _dot(mix, wout_ref[...])


def _even_kernel(r_ref, *rest):
    params, o_ref, (buf_ref, unperm_ref) = rest[:-3], rest[-3], rest[-2:]

    @pl.when(pl.program_id(1) == 0)
    def _():
        buf_ref[:, 0:SUBLANES, :] = jnp.zeros((B_WIDTH // LANES, SUBLANES, LANES), _f32)

    _for_each_tile(r_ref.shape[0], lambda rows: _even_tile(
        r_ref.at[rows], *params, o_ref.at[rows], buf_ref, unperm_ref))


def _odd_tile(r_ref, nrm_ref, win_ref, bin_ref, dww_ref, dwb_ref, lng_ref, lnb_ref, wout_ref,
              bout_ref, o_ref, ybuf_ref, conv_ref, unperm_ref):
    ts = SEQ_TILE
    n_slabs = C_WIDTH // LANES
    n_groups = C_WIDTH // C_GROUP
    n = ts // C_PHASES
    base = C_HALO - (C_CONV - 1)

    def glu(t, h, q):
        z_v, z_g = [_dot(h, win_ref[:, c0:c0 + C_GROUP]) + bin_ref[:, c0:c0 + C_GROUP]
                    for c0 in (q * C_GROUP, C_WIDTH + q * C_GROUP)]
        y = z_v * jax.nn.sigmoid(z_g)
        for j in range(C_GROUP // LANES):
            s = q * (C_GROUP // LANES) + j
            ybuf_ref[s, C_HALO + t * ts:C_HALO + (t + 1) * ts, :] = y[:, j * LANES:(j + 1) * LANES]

    def conv(t, s):
        slab = ybuf_ref.at[s]
        lanes = slice(s * LANES, (s + 1) * LANES)
        taps = [jnp.broadcast_to(dww_ref[k:k + 1, lanes], (C_ROWS, LANES)) for k in range(C_CONV)]
        bias = jnp.broadcast_to(dwb_ref[:, lanes], (C_ROWS, LANES))
        for mb in range(n // C_ROWS):
            accs = [bias] * C_PHASES
            for i in range(C_PHASES + C_CONV - 1):
                yi = slab[pl.ds(base + t * ts + i + C_PHASES * C_ROWS * mb, C_ROWS,
                                stride=C_PHASES), :]
                for ph in range(C_PHASES):
                    k = i - ph
                    if 0 <= k < C_CONV:
                        accs[ph] = accs[ph] + yi * taps[k]
            for ph in range(C_PHASES):
                conv_ref[t, ph * n + mb * C_ROWS:ph * n + (mb + 1) * C_ROWS, lanes] = accs[ph]

    def finish(t):
        y = _layernorm(conv_ref[t], lng_ref[...], lnb_ref[...])
        y = _natural_rows(y * jax.nn.sigmoid(y), unperm_ref.at[t], C_PHASES).astype(_bf16)
        o_ref[t * ts:(t + 1) * ts, :] = (r_ref[t * ts:(t + 1) * ts, :] + _dot(y, wout_ref[...])
                                         + bout_ref[...])

    for t in range(C_TILES):
        h = _rms(r_ref[t * ts:(t + 1) * ts, :], nrm_ref[...]).astype(_bf16)
        glu(t, h, 0)
        for q in range(n_groups):
            if q + 1 < n_groups:
                glu(t, h, q + 1)
            if q == 0 and t > 0:
                finish(t - 1)
            for j in range(C_GROUP // LANES):
                conv(t, q * (C_GROUP // LANES) + j)
    finish(C_TILES - 1)
    for s in range(n_slabs):
        ybuf_ref[s, 0:C_HALO, :] = ybuf_ref[s, C_TILES * ts:C_TILES * ts + C_HALO, :]


def _odd_kernel(r_ref, *rest):
    params, o_ref, (ybuf_ref, conv_ref, unperm_ref) = rest[:-4], rest[-4], rest[-3:]

    @pl.when(pl.program_id(1) == 0)
    def _():
        ybuf_ref[:, 0:C_HALO, :] = jnp.zeros((C_WIDTH // LANES, C_HALO, LANES), _f32)

    _for_each_tile(r_ref.shape[0], lambda rows: _odd_tile(
        r_ref.at[rows], *params, o_ref.at[rows], ybuf_ref, conv_ref, unperm_ref),
        tile_rows=C_TILES * SEQ_TILE)


def _ffn_tile(r_ref, p_ref, nrm_ref, wup_ref, dww_ref, dwb_ref, wdn_ref, plen_ref, wg_ref, wp_ref,
              fin_ref, o_ref, carry_ref, buf_ref, unperm_ref, *, final):
    ts = r_ref.shape[0]

    pe = _dot(p_ref[...].astype(_bf16), wp_ref[...])
    r = r_ref[...]
    h = _rms(r, nrm_ref[...]).astype(_bf16)
    n_slabs = FF_CHUNK // LANES

    def up(c):
        for half in range(2):
            col0 = half * D_FF + c * FF_CHUNK
            z = _dot(h, wup_ref[:, col0:col0 + FF_CHUNK])
            for s in range(n_slabs):
                slab = buf_ref.at[c % FF_SLOTS, half, s]
                lo = col0 + s * LANES
                slab[0:SUBLANES, :] = carry_ref[:, lo:lo + LANES]
                slab[SUBLANES:SUBLANES + ts, :] = z[:, s * LANES:(s + 1) * LANES]

    def gated(c):
        cols = []
        for s in range(n_slabs):
            convs = []
            for half in range(2):
                lo = half * D_FF + c * FF_CHUNK + s * LANES
                slab = buf_ref.at[c % FF_SLOTS, half, s]
                carry_ref[:, lo:lo + LANES] = slab[ts:ts + SUBLANES, :]
                convs.append(_conv3_phased(slab, ts, dww_ref[:, lo:lo + LANES])
                             + dwb_ref[:, lo:lo + LANES])
            g, u = convs
            cols.append(g * jax.nn.sigmoid(g) * u)
        return jnp.concatenate(cols, axis=1).astype(_bf16)

    for c in range(FF_SLOTS - 1):
        up(c)
    acc = None
    for c in range(N_FF_CHUNKS):
        if c + FF_SLOTS - 1 < N_FF_CHUNKS:
            up(c + FF_SLOTS - 1)
        d = _dot(gated(c), wdn_ref[c * FF_CHUNK:(c + 1) * FF_CHUNK, :])
        acc = d if acc is None else acc + d
    r2 = r + _natural_rows(acc, unperm_ref)

    half = ts // 2
    for lo in (0, half):
        r2h = r2[lo:lo + half]
        gate = jax.nn.sigmoid(_dot(_rms(r2h, plen_ref[...]).astype(_bf16), wg_ref[...]))
        r3 = r2h + gate * pe[lo:lo + half]
        if final:
            r3 = _rms(r3, fin_ref[...])
        o_ref[lo:lo + half, :] = r3


def _ffn_kernel(r_ref, p_ref, *rest, final):
    params, o_ref, (carry_ref, buf_ref, unperm_ref) = rest[:-4], rest[-4], rest[-3:]

    @pl.when(pl.program_id(1) == 0)
    def _():
        carry_ref[...] = jnp.zeros_like(carry_ref)

    _for_each_tile(r_ref.shape[0], lambda rows: _ffn_tile(
        r_ref.at[rows], p_ref.at[rows], *params, o_ref.at[rows], carry_ref, buf_ref, unperm_ref,
        final=final))


def _rows(n, layer=None):
    if layer is None:
        return pl.BlockSpec((None, SEQ_BLOCK, n), lambda b, j: (b, j, 0))
    return pl.BlockSpec((None, None, SEQ_BLOCK, n), lambda b, j: (layer, b, j, 0))


class _Layer(NamedTuple):
    stacked: jax.Array
    layer: int


def _whole(a):
    if isinstance(a, _Layer):
        zeros = (0,) * (a.stacked.ndim - 1)
        return pl.BlockSpec((None,) + a.stacked.shape[1:], lambda b, j: (a.layer,) + zeros,
                            pipeline_mode=pl.Buffered(1))
    nd = a.ndim
    return pl.BlockSpec(a.shape, lambda b, j: (0,) * nd, pipeline_mode=pl.Buffered(1))


def _call(body, name, batch, seq, row_inputs, params, scratch, row_layers=None):
    grid = (batch, seq // SEQ_BLOCK)
    row_layers = row_layers or [None] * len(row_inputs)
    return pl.pallas_call(
        body,
        name=name,
        grid=grid,
        in_specs=([_rows(a.shape[-1], layer) for a, layer in zip(row_inputs, row_layers)]
                  + [_whole(a) for a in params]),
        out_specs=_rows(D_MODEL),
        out_shape=jax.ShapeDtypeStruct((batch, seq, D_MODEL), _f32),
        scratch_shapes=scratch,
        compiler_params=pltpu.CompilerParams(
            dimension_semantics=("arbitrary", "arbitrary"),
            vmem_limit_bytes=VMEM_LIMIT_BYTES),
    )(*row_inputs, *[a.stacked if isinstance(a, _Layer) else a for a in params])


def _row(v):
    return v.reshape(1, -1).astype(_f32)


def _cast_kernel(*refs):
    n = len(refs) // 2
    for x_ref, o_ref in zip(refs[:n], refs[n:]):
        o_ref[...] = x_ref[...].astype(o_ref.dtype)


def _to_bf16(ws):
    def spec(w):
        layers, rows, cols = w.shape
        return pl.BlockSpec((layers, rows // CAST_STEPS, cols), lambda i: (0, i, 0))

    return pl.pallas_call(
        _cast_kernel,
        name="to_bf16",
        grid=(CAST_STEPS,),
        in_specs=[spec(w) for w in ws],
        out_specs=[spec(w) for w in ws],
        out_shape=[jax.ShapeDtypeStruct(w.shape, _bf16) for w in ws],
        compiler_params=pltpu.CompilerParams(vmem_limit_bytes=CAST_VMEM_LIMIT_BYTES),
    )(*ws)


def _even_layer(r, norm, w_in, a_ws, a_bs, a_ln_g, a_ln_b, b_conv_w, w_out):
    batch, seq, _ = r.shape
    params = [_row(norm), w_in, a_ws, a_bs.T, _row(a_ln_g), _row(a_ln_b), b_conv_w, w_out]
    scratch = [pltpu.VMEM((B_WIDTH // LANES, SUBLANES + SEQ_TILE, LANES), _f32),
               pltpu.VMEM((B_WIDTH // LANES, SEQ_TILE, LANES), _f32)]
    return _call(_even_kernel, "even_mixer", batch, seq, [r], params, scratch)


def _odd_layer(r, norm, w_in, b_in, dw_w, dw_b, ln_g, ln_b, w_out, b_out):
    batch, seq, _ = r.shape
    params = [_row(norm), w_in, _row(b_in), dw_w, _row(dw_b), _row(ln_g), _row(ln_b), w_out,
              _row(b_out)]
    scratch = [pltpu.VMEM((C_WIDTH // LANES, C_HALO + C_TILES * SEQ_TILE, LANES), _f32),
               pltpu.VMEM((C_TILES, SEQ_TILE, C_WIDTH), _f32),
               pltpu.VMEM((C_TILES, C_WIDTH // LANES, SEQ_TILE, LANES), _f32)]
    return _call(_odd_kernel, "odd_mixer", batch, seq, [r], params, scratch)


def _ffn_layer(r, p, layer, norm, w_up, dw_w, dw_b, w_down, ple_norm, w_g, w_p, final_norm, final):
    batch, seq, _ = r.shape
    params = [_row(norm), w_up, dw_w, _row(dw_b), w_down, _row(ple_norm), w_g, w_p,
              _row(final_norm)]
    scratch = [pltpu.VMEM((SUBLANES, 2 * D_FF), _f32),
               pltpu.VMEM((FF_SLOTS, 2, FF_CHUNK // LANES, SUBLANES + SEQ_TILE, LANES), _f32),
               pltpu.VMEM((D_MODEL // LANES, SEQ_TILE, LANES), _f32)]
    return _call(functools.partial(_ffn_kernel, final=final), "ffn_final" if final else "ffn",
                 batch, seq, [r, p], params, scratch, row_layers=[None, layer])


def kernel(x, p, ev_norm, ev_w_in, ev_a_ws, ev_a_bs, ev_a_ln_g, ev_a_ln_b, ev_b_conv_w, ev_w_out,
           od_norm, od_w_in, od_b_in, od_dw_w, od_dw_b, od_ln_g, od_ln_b, od_w_out, od_b_out,
           ffn_norm, ffn_w_up, ffn_dw_w, ffn_dw_b, ffn_w_down, ple_w_p, ple_norm, ple_w_g,
           final_norm):
    depth = p.shape[0]
    ev_w_in, ev_w_out, od_w_in, od_w_out, ffn_w_up, ffn_w_down, ple_w_g, ple_w_p = _to_bf16(
        [ev_w_in, ev_w_out, od_w_in, od_w_out, ffn_w_up, ffn_w_down, ple_w_g, ple_w_p])
    r = x
    for i in range(depth):
        j = i // 2
        if i % 2 == 0:
            r = _even_layer(r, ev_norm[j], _Layer(ev_w_in, j), ev_a_ws[j], ev_a_bs[j],
                            ev_a_ln_g[j], ev_a_ln_b[j], ev_b_conv_w[j], _Layer(ev_w_out, j))
        else:
            r = _odd_layer(r, od_norm[j], _Layer(od_w_in, j), od_b_in[j], od_dw_w[j], od_dw_b[j],
                           od_ln_g[j], od_ln_b[j], _Layer(od_w_out, j), od_b_out[j])
        r = _ffn_layer(r, p, i, ffn_norm[i], _Layer(ffn_w_up, i), ffn_dw_w[i], ffn_dw_b[i],
                       _Layer(ffn_w_down, i), ple_norm[i], _Layer(ple_w_g, i),
                       _Layer(ple_w_p, i), final_norm, final=(i == depth - 1))
    return r
```

```python
import functools
from typing import NamedTuple

import jax
import jax.numpy as jnp
from jax import lax
from jax.experimental import pallas as pl
from jax.experimental.pallas import tpu as pltpu

D_MODEL = 1024
PLE_DIM = 256
CHUNK = 128
A_HEADS = 4
A_HEAD_DIM = 128
A_WIDTH = A_HEADS * A_HEAD_DIM
B_WIDTH = 512
IN_WIDTH = 2 * A_WIDTH + 3 * B_WIDTH
C_WIDTH = D_MODEL
C_CONV = 31
D_FF = 2816
FFN_CONV = 3
EPS = 1e-6

SUBLANES = 8
LANES = 128
ROW_PHASES = 2
SEQ_TILE = 512
SEQ_BLOCK = 1024
FF_CHUNK = 256
N_FF_CHUNKS = D_FF // FF_CHUNK
FF_SLOTS = 4
C_HALO = 32
C_PHASES = 2
C_ROWS = 128
C_GROUP = 256
CAST_STEPS = 16
CAST_VMEM_LIMIT_BYTES = 40 * 1024 * 1024
VMEM_LIMIT_BYTES = 56 * 1024 * 1024

assert D_FF % FF_CHUNK == 0 and SEQ_TILE % CHUNK == 0 and C_HALO >= C_CONV - 1
assert SEQ_BLOCK % SEQ_TILE == 0

_bf16 = jnp.bfloat16
_f32 = jnp.float32


def _dot(a, b):
    return jnp.dot(a, b, preferred_element_type=_f32)


def _rms(x, g):
    ms = jnp.mean(x * x, axis=-1, keepdims=True)
    return x * lax.rsqrt(ms + EPS) * g


def _layernorm(x, g, b):
    mu = jnp.mean(x, axis=-1, keepdims=True)
    xc = x - mu
    var = jnp.mean(xc * xc, axis=-1, keepdims=True)
    return xc * lax.rsqrt(var + EPS) * g + b


def _for_each_tile(n_rows, tile):
    def body(i, carry):
        tile(pl.ds(pl.multiple_of(i * SEQ_TILE, SEQ_TILE), SEQ_TILE))
        return carry

    lax.fori_loop(0, n_rows // SEQ_TILE, body, 0)


def _conv3_phased(slab_ref, t, w):
    n = t // ROW_PHASES
    base = SUBLANES - (FFN_CONV - 1)
    taps = [slab_ref[pl.ds(base + i, n, stride=ROW_PHASES), :]
            for i in range(ROW_PHASES + FFN_CONV - 1)]
    return jnp.concatenate(
        [sum(taps[ph + k] * w[k:k + 1, :] for k in range(FFN_CONV)) for ph in range(ROW_PHASES)],
        axis=0)


def _natural_rows(x, slabs_ref, phases=ROW_PHASES):
    t = x.shape[0]
    n = t // phases
    n_slabs = x.shape[1] // LANES
    for s in range(n_slabs):
        for ph in range(phases):
            slabs_ref[s, pl.ds(ph, n, stride=phases), :] = (
                x[ph * n:(ph + 1) * n, s * LANES:(s + 1) * LANES])
    return jnp.concatenate([slabs_ref[s] for s in range(n_slabs)], axis=1)


def _even_tile(r_ref, nrm_ref, win_ref, ws_ref, bs_ref, lng_ref, lnb_ref, cw_ref, wout_ref,
               o_ref, buf_ref, unperm_ref):
    ts = r_ref.shape[0]
    n_chunks = ts // CHUNK

    r = r_ref[...]
    h = _rms(r, nrm_ref[...]).astype(_bf16)
    a_v = jax.nn.gelu(_dot(h, win_ref[:, A_WIDTH:2 * A_WIDTH]))
    a_u = jax.nn.gelu(_dot(h, win_ref[:, 0:A_WIDTH]))
    z_b = _dot(h, win_ref[:, 2 * A_WIDTH:IN_WIDTH])
    b_b = z_b[:, 0:B_WIDTH]
    b_c = z_b[:, B_WIDTH:2 * B_WIDTH]
    b_h = z_b[:, 2 * B_WIDTH:3 * B_WIDTH]

    row = lax.broadcasted_iota(jnp.int32, (CHUNK, CHUNK), 0)
    col = lax.broadcasted_iota(jnp.int32, (CHUNK, CHUNK), 1)
    tril = row >= col
    a_parts = []
    for hd in range(A_HEADS):
        lo = hd * A_HEAD_DIM
        vh = _layernorm(a_v[:, lo:lo + A_HEAD_DIM], lng_ref[:, lo:lo + A_HEAD_DIM],
                        lnb_ref[:, lo:lo + A_HEAD_DIM]).astype(_bf16)
        rhs = jnp.concatenate([vh[n * CHUNK:(n + 1) * CHUNK, :] for n in range(n_chunks)], axis=1)
        w = jnp.where(tril, ws_ref[hd], 0.0).astype(_bf16)
        mixed = _dot(w, rhs) + bs_ref[:, hd:hd + 1]
        mixed = jnp.concatenate(
            [mixed[:, n * A_HEAD_DIM:(n + 1) * A_HEAD_DIM] for n in range(n_chunks)], axis=0)
        a_parts.append(a_u[:, lo:lo + A_HEAD_DIM] * mixed)

    x = b_c * b_h
    convs = []
    for s in range(B_WIDTH // LANES):
        slab = buf_ref.at[s]
        slab[SUBLANES:SUBLANES + ts, :] = x[:, s * LANES:(s + 1) * LANES]
        convs.append(_conv3_phased(slab, ts, cw_ref[:, s * LANES:(s + 1) * LANES]))
        slab[0:SUBLANES, :] = slab[ts:ts + SUBLANES, :]
    b_out = b_b * _natural_rows(jnp.concatenate(convs, axis=1), unperm_ref)

    mix = jnp.concatenate(a_parts + [b_out], axis=1).astype(_bf16)
    o_ref[...] = r + _dot(mix, wout_ref[...])


def _even_kernel(r_ref, *rest):
    params, o_ref, (buf_ref, unperm_ref) = rest[:-3], rest[-3], rest[-2:]

    @pl.when(pl.program_id(1) == 0)
    def _():
        buf_ref[:, 0:SUBLANES, :] = jnp.zeros((B_WIDTH // LANES, SUBLANES, LANES), _f32)

    _for_each_tile(r_ref.shape[0], lambda rows: _even_tile(
        r_ref.at[rows], *params, o_ref.at[rows], buf_ref, unperm_ref))


def _odd_tile(r_ref, nrm_ref, win_ref, bin_ref, dww_ref, dwb_ref, lng_ref, lnb_ref, wout_ref,
              bout_ref, o_ref, ybuf_ref, conv_ref, unperm_ref):
    ts = r_ref.shape[0]
    n_slabs = C_WIDTH // LANES
    n_groups = C_WIDTH // C_GROUP
    n = ts // C_PHASES
    base = C_HALO - (C_CONV - 1)

    r = r_ref[...]
    h = _rms(r, nrm_ref[...]).astype(_bf16)

    def glu(q):
        z_v, z_g = [_dot(h, win_ref[:, c0:c0 + C_GROUP]) + bin_ref[:, c0:c0 + C_GROUP]
                    for c0 in (q * C_GROUP, C_WIDTH + q * C_GROUP)]
        y = z_v * jax.nn.sigmoid(z_g)
        for j in range(C_GROUP // LANES):
            s = q * (C_GROUP // LANES) + j
            ybuf_ref[s, C_HALO:C_HALO + ts, :] = y[:, j * LANES:(j + 1) * LANES]

    def conv(s):
        slab = ybuf_ref.at[s]
        lanes = slice(s * LANES, (s + 1) * LANES)
        taps = [jnp.broadcast_to(dww_ref[k:k + 1, lanes], (C_ROWS, LANES)) for k in range(C_CONV)]
        bias = jnp.broadcast_to(dwb_ref[:, lanes], (C_ROWS, LANES))
        for mb in range(n // C_ROWS):
            accs = [bias] * C_PHASES
            for i in range(C_PHASES + C_CONV - 1):
                yi = slab[pl.ds(base + i + C_PHASES * C_ROWS * mb, C_ROWS, stride=C_PHASES), :]
                for ph in range(C_PHASES):
                    k = i - ph
                    if 0 <= k < C_CONV:
                        accs[ph] = accs[ph] + yi * taps[k]
            for ph in range(C_PHASES):
                conv_ref[ph * n + mb * C_ROWS:ph * n + (mb + 1) * C_ROWS, lanes] = accs[ph]
        slab[0:C_HALO, :] = slab[ts:ts + C_HALO, :]

    glu(0)
    for q in range(n_groups):
        if q + 1 < n_groups:
            glu(q + 1)
        for j in range(C_GROUP // LANES):
            conv(q * (C_GROUP // LANES) + j)

    y = _layernorm(conv_ref[...], lng_ref[...], lnb_ref[...])
    y = _natural_rows(y * jax.nn.sigmoid(y), unperm_ref, C_PHASES).astype(_bf16)
    o_ref[...] = r + _dot(y, wout_ref[...]) + bout_ref[...]


def _odd_kernel(r_ref, *rest):
    params, o_ref, (ybuf_ref, conv_ref, unperm_ref) = rest[:-4], rest[-4], rest[-3:]

    @pl.when(pl.program_id(1) == 0)
    def _():
        ybuf_ref[:, 0:C_HALO, :] = jnp.zeros((C_WIDTH // LANES, C_HALO, LANES), _f32)

    _for_each_tile(r_ref.shape[0], lambda rows: _odd_tile(
        r_ref.at[rows], *params, o_ref.at[rows], ybuf_ref, conv_ref, unperm_ref))


def _ffn_tile(r_ref, p_ref, nrm_ref, wup_ref, dww_ref, dwb_ref, wdn_ref, plen_ref, wg_ref, wp_ref,
              fin_ref, o_ref, carry_ref, buf_ref, unperm_ref, *, final):
    ts = r_ref.shape[0]

    pe = _dot(p_ref[...].astype(_bf16), wp_ref[...])
    r = r_ref[...]
    h = _rms(r, nrm_ref[...]).astype(_bf16)
    n_slabs = FF_CHUNK // LANES

    def up(c):
        for half in range(2):
            col0 = half * D_FF + c * FF_CHUNK
            z = _dot(h, wup_ref[:, col0:col0 + FF_CHUNK])
            for s in range(n_slabs):
                slab = buf_ref.at[c % FF_SLOTS, half, s]
                lo = col0 + s * LANES
                slab[0:SUBLANES, :] = carry_ref[:, lo:lo + LANES]
                slab[SUBLANES:SUBLANES + ts, :] = z[:, s * LANES:(s + 1) * LANES]

    def gated(c):
        cols = []
        for s in range(n_slabs):
            convs = []
            for half in range(2):
                lo = half * D_FF + c * FF_CHUNK + s * LANES
                slab = buf_ref.at[c % FF_SLOTS, half, s]
                carry_ref[:, lo:lo + LANES] = slab[ts:ts + SUBLANES, :]
                convs.append(_conv3_phased(slab, ts, dww_ref[:, lo:lo + LANES])
                             + dwb_ref[:, lo:lo + LANES])
            g, u = convs
            cols.append(g * jax.nn.sigmoid(g) * u)
        return jnp.concatenate(cols, axis=1).astype(_bf16)

    for c in range(FF_SLOTS - 1):
        up(c)
    acc = None
    for c in range(N_FF_CHUNKS):
        if c + FF_SLOTS - 1 < N_FF_CHUNKS:
            up(c + FF_SLOTS - 1)
        d = _dot(gated(c), wdn_ref[c * FF_CHUNK:(c + 1) * FF_CHUNK, :])
        acc = d if acc is None else acc + d
    r2 = r + _natural_rows(acc, unperm_ref)

    half = ts // 2
    for lo in (0, half):
        r2h = r2[lo:lo + half]
        gate = jax.nn.sigmoid(_dot(_rms(r2h, plen_ref[...]).astype(_bf16), wg_ref[...]))
        r3 = r2h + gate * pe[lo:lo + half]
        if final:
            r3 = _rms(r3, fin_ref[...])
        o_ref[lo:lo + half, :] = r3


def _ffn_kernel(r_ref, p_ref, *rest, final):
    params, o_ref, (carry_ref, buf_ref, unperm_ref) = rest[:-4], rest[-4], rest[-3:]

    @pl.when(pl.program_id(1) == 0)
    def _():
        carry_ref[...] = jnp.zeros_like(carry_ref)

    _for_each_tile(r_ref.shape[0], lambda rows: _ffn_tile(
        r_ref.at[rows], p_ref.at[rows], *params, o_ref.at[rows], carry_ref, buf_ref, unperm_ref,
        final=final))


def _rows(n, layer=None):
    if layer is None:
        return pl.BlockSpec((None, SEQ_BLOCK, n), lambda b, j: (b, j, 0))
    return pl.BlockSpec((None, None, SEQ_BLOCK, n), lambda b, j: (layer, b, j, 0))


class _Layer(NamedTuple):
    stacked: jax.Array
    layer: int


def _whole(a):
    if isinstance(a, _Layer):
        zeros = (0,) * (a.stacked.ndim - 1)
        return pl.BlockSpec((None,) + a.stacked.shape[1:], lambda b, j: (a.layer,) + zeros,
                            pipeline_mode=pl.Buffered(1))
    nd = a.ndim
    return pl.BlockSpec(a.shape, lambda b, j: (0,) * nd, pipeline_mode=pl.Buffered(1))


def _call(body, name, batch, seq, row_inputs, params, scratch, row_layers=None):
    grid = (batch, seq // SEQ_BLOCK)
    row_layers = row_layers or [None] * len(row_inputs)
    return pl.pallas_call(
        body,
        name=name,
        grid=grid,
        in_specs=([_rows(a.shape[-1], layer) for a, layer in zip(row_inputs, row_layers)]
                  + [_whole(a) for a in params]),
        out_specs=_rows(D_MODEL),
        out_shape=jax.ShapeDtypeStruct((batch, seq, D_MODEL), _f32),
        scratch_shapes=scratch,
        compiler_params=pltpu.CompilerParams(
            dimension_semantics=("arbitrary", "arbitrary"),
            vmem_limit_bytes=VMEM_LIMIT_BYTES),
    )(*row_inputs, *[a.stacked if isinstance(a, _Layer) else a for a in params])


def _row(v):
    return v.reshape(1, -1).astype(_f32)


def _cast_kernel(*refs):
    n = len(refs) // 2
    for x_ref, o_ref in zip(refs[:n], refs[n:]):
        o_ref[...] = x_ref[...].astype(o_ref.dtype)


def _to_bf16(ws):
    def spec(w):
        layers, rows, cols = w.shape
        return pl.BlockSpec((layers, rows // CAST_STEPS, cols), lambda i: (0, i, 0))

    return pl.pallas_call(
        _cast_kernel,
        name="to_bf16",
        grid=(CAST_STEPS,),
        in_specs=[spec(w) for w in ws],
        out_specs=[spec(w) for w in ws],
        out_shape=[jax.ShapeDtypeStruct(w.shape, _bf16) for w in ws],
        compiler_params=pltpu.CompilerParams(vmem_limit_bytes=CAST_VMEM_LIMIT_BYTES),
    )(*ws)


def _even_layer(r, norm, w_in, a_ws, a_bs, a_ln_g, a_ln_b, b_conv_w, w_out):
    batch, seq, _ = r.shape
    params = [_row(norm), w_in, a_ws, a_bs.T, _row(a_ln_g), _row(a_ln_b), b_conv_w, w_out]
    scratch = [pltpu.VMEM((B_WIDTH // LANES, SUBLANES + SEQ_TILE, LANES), _f32),
               pltpu.VMEM((B_WIDTH // LANES, SEQ_TILE, LANES), _f32)]
    return _call(_even_kernel, "even_mixer", batch, seq, [r], params, scratch)


def _odd_layer(r, norm, w_in, b_in, dw_w, dw_b, ln_g, ln_b, w_out, b_out):
    batch, seq, _ = r.shape
    params = [_row(norm), w_in, _row(b_in), dw_w, _row(dw_b), _row(ln_g), _row(ln_b), w_out,
              _row(b_out)]
    scratch = [pltpu.VMEM((C_WIDTH // LANES, C_HALO + SEQ_TILE, LANES), _f32),
               pltpu.VMEM((SEQ_TILE, C_WIDTH), _f32),
               pltpu.VMEM((C_WIDTH // LANES, SEQ_TILE, LANES), _f32)]
    return _call(_odd_kernel, "odd_mixer", batch, seq, [r], params, scratch)


def _ffn_layer(r, p, layer, norm, w_up, dw_w, dw_b, w_down, ple_norm, w_g, w_p, final_norm, final):
    batch, seq, _ = r.shape
    params = [_row(norm), w_up, dw_w, _row(dw_b), w_down, _row(ple_norm), w_g, w_p,
              _row(final_norm)]
    scratch = [pltpu.VMEM((SUBLANES, 2 * D_FF), _f32),
               pltpu.VMEM((FF_SLOTS, 2, FF_CHUNK // LANES, SUBLANES + SEQ_TILE, LANES), _f32),
               pltpu.VMEM((D_MODEL // LANES, SEQ_TILE, LANES), _f32)]
    return _call(functools.partial(_ffn_kernel, final=final), "ffn_final" if final else "ffn",
                 batch, seq, [r, p], params, scratch, row_layers=[None, layer])


def kernel(x, p, ev_norm, ev_w_in, ev_a_ws, ev_a_bs, ev_a_ln_g, ev_a_ln_b, ev_b_conv_w, ev_w_out,
           od_norm, od_w_in, od_b_in, od_dw_w, od_dw_b, od_ln_g, od_ln_b, od_w_out, od_b_out,
           ffn_norm, ffn_w_up, ffn_dw_w, ffn_dw_b, ffn_w_down, ple_w_p, ple_norm, ple_w_g,
           final_norm):
    depth = p.shape[0]
    ev_w_in, ev_w_out, od_w_in, od_w_out, ffn_w_up, ffn_w_down, ple_w_g, ple_w_p = _to_bf16(
        [ev_w_in, ev_w_out, od_w_in, od_w_out, ffn_w_up, ffn_w_down, ple_w_g, ple_w_p])
    r = x
    for i in range(depth):
        j = i // 2
        if i % 2 == 0:
            r = _even_layer(r, ev_norm[j], _Layer(ev_w_in, j), ev_a_ws[j], ev_a_bs[j],
                            ev_a_ln_g[j], ev_a_ln_b[j], ev_b_conv_w[j], _Layer(ev_w_out, j))
        else:
            r = _odd_layer(r, od_norm[j], _Layer(od_w_in, j), od_b_in[j], od_dw_w[j], od_dw_b[j],
                           od_ln_g[j], od_ln_b[j], _Layer(od_w_out, j), od_b_out[j])
        r = _ffn_layer(r, p, i, ffn_norm[i], _Layer(ffn_w_up, i), ffn_dw_w[i], ffn_dw_b[i],
                       _Layer(ffn_w_down, i), ple_norm[i], _Layer(ple_w_g, i),
                       _Layer(ple_w_p, i), final_norm, final=(i == depth - 1))
    return r
```

```python
import functools
from typing import NamedTuple

import jax
import jax.numpy as jnp
from jax import lax
from jax.experimental import pallas as pl
from jax.experimental.pallas import tpu as pltpu

D_MODEL = 1024
PLE_DIM = 256
CHUNK = 128
A_HEADS = 4
A_HEAD_DIM = 128
A_WIDTH = A_HEADS * A_HEAD_DIM
B_WIDTH = 512
IN_WIDTH = 2 * A_WIDTH + 3 * B_WIDTH
C_WIDTH = D_MODEL
C_CONV = 31
D_FF = 2816
FFN_CONV = 3
EPS = 1e-6

SUBLANES = 8
LANES = 128
ROW_PHASES = 2
SEQ_TILE = 512
SEQ_BLOCK = 1024
MERGED_TILE = 256
MERGED_BLOCK = 512
FF_CHUNK = 256
N_FF_CHUNKS = D_FF // FF_CHUNK
FF_SLOTS = 4
C_HALO = 32
C_PHASES = 2
C_ROWS = 128
C_GROUP = 256
CAST_STEPS = 16
CAST_VMEM_LIMIT_BYTES = 40 * 1024 * 1024
VMEM_LIMIT_BYTES = 56 * 1024 * 1024

assert D_FF % FF_CHUNK == 0 and SEQ_TILE % CHUNK == 0 and C_HALO >= C_CONV - 1
assert SEQ_BLOCK % SEQ_TILE == 0

_bf16 = jnp.bfloat16
_f32 = jnp.float32


def _dot(a, b):
    return jnp.dot(a, b, preferred_element_type=_f32)


def _rms(x, g):
    ms = jnp.mean(x * x, axis=-1, keepdims=True)
    return x * lax.rsqrt(ms + EPS) * g


def _layernorm(x, g, b):
    mu = jnp.mean(x, axis=-1, keepdims=True)
    xc = x - mu
    var = jnp.mean(xc * xc, axis=-1, keepdims=True)
    return xc * lax.rsqrt(var + EPS) * g + b


def _for_each_tile(n_rows, tile, tile_rows=SEQ_TILE):
    def body(i, carry):
        tile(pl.ds(pl.multiple_of(i * tile_rows, tile_rows), tile_rows))
        return carry

    lax.fori_loop(0, n_rows // tile_rows, body, 0)


def _conv3_phased(slab_ref, t, w):
    n = t // ROW_PHASES
    base = SUBLANES - (FFN_CONV - 1)
    taps = [slab_ref[pl.ds(base + i, n, stride=ROW_PHASES), :]
            for i in range(ROW_PHASES + FFN_CONV - 1)]
    return jnp.concatenate(
        [sum(taps[ph + k] * w[k:k + 1, :] for k in range(FFN_CONV)) for ph in range(ROW_PHASES)],
        axis=0)


def _natural_rows(x, slabs_ref, phases=ROW_PHASES):
    t = x.shape[0]
    n = t // phases
    n_slabs = x.shape[1] // LANES
    for s in range(n_slabs):
        for ph in range(phases):
            slabs_ref[s, pl.ds(ph, n, stride=phases), :] = (
                x[ph * n:(ph + 1) * n, s * LANES:(s + 1) * LANES])
    return jnp.concatenate([slabs_ref[s] for s in range(n_slabs)], axis=1)


def _even_tile(r_ref, nrm_ref, win_ref, ws_ref, bs_ref, lng_ref, lnb_ref, cw_ref, wout_ref,
               o_ref, buf_ref, unperm_ref):
    ts = r_ref.shape[0]
    n_chunks = ts // CHUNK

    r = r_ref[...]
    h = _rms(r, nrm_ref[...]).astype(_bf16)
    a_v = jax.nn.gelu(_dot(h, win_ref[:, A_WIDTH:2 * A_WIDTH]))
    a_u = jax.nn.gelu(_dot(h, win_ref[:, 0:A_WIDTH]))
    z_b = _dot(h, win_ref[:, 2 * A_WIDTH:IN_WIDTH])
    b_b = z_b[:, 0:B_WIDTH]
    b_c = z_b[:, B_WIDTH:2 * B_WIDTH]
    b_h = z_b[:, 2 * B_WIDTH:3 * B_WIDTH]

    row = lax.broadcasted_iota(jnp.int32, (CHUNK, CHUNK), 0)
    col = lax.broadcasted_iota(jnp.int32, (CHUNK, CHUNK), 1)
    tril = row >= col
    a_parts = []
    for hd in range(A_HEADS):
        lo = hd * A_HEAD_DIM
        vh = _layernorm(a_v[:, lo:lo + A_HEAD_DIM], lng_ref[:, lo:lo + A_HEAD_DIM],
                        lnb_ref[:, lo:lo + A_HEAD_DIM]).astype(_bf16)
        rhs = jnp.concatenate([vh[n * CHUNK:(n + 1) * CHUNK, :] for n in range(n_chunks)], axis=1)
        w = jnp.where(tril, ws_ref[hd], 0.0).astype(_bf16)
        mixed = _dot(w, rhs) + bs_ref[:, hd:hd + 1]
        mixed = jnp.concatenate(
            [mixed[:, n * A_HEAD_DIM:(n + 1) * A_HEAD_DIM] for n in range(n_chunks)], axis=0)
        a_parts.append(a_u[:, lo:lo + A_HEAD_DIM] * mixed)

    x = b_c * b_h
    convs = []
    for s in range(B_WIDTH // LANES):
        slab = buf_ref.at[s]
        slab[SUBLANES:SUBLANES + ts, :] = x[:, s * LANES:(s + 1) * LANES]
        convs.append(_conv3_phased(slab, ts, cw_ref[:, s * LANES:(s + 1) * LANES]))
        slab[0:SUBLANES, :] = slab[ts:ts + SUBLANES, :]
    b_out = b_b * _natural_rows(jnp.concatenate(convs, axis=1), unperm_ref)

    mix = jnp.concatenate(a_parts + [b_out], axis=1).astype(_bf16)
    o_ref[...] = r + _dot(mix, wout_ref[...])


def _even_kernel(r_ref, *rest):
    params, o_ref, (buf_ref, unperm_ref) = rest[:-3], rest[-3], rest[-2:]

    @pl.when(pl.program_id(1) == 0)
    def _():
        buf_ref[:, 0:SUBLANES, :] = jnp.zeros((B_WIDTH // LANES, SUBLANES, LANES), _f32)

    _for_each_tile(r_ref.shape[0], lambda rows: _even_tile(
        r_ref.at[rows], *params, o_ref.at[rows], buf_ref, unperm_ref))


def _odd_tile(r_ref, nrm_ref, win_ref, bin_ref, dww_ref, dwb_ref, lng_ref, lnb_ref, wout_ref,
              bout_ref, o_ref, ybuf_ref, conv_ref, unperm_ref):
    ts = r_ref.shape[0]
    n_slabs = C_WIDTH // LANES
    n_groups = C_WIDTH // C_GROUP
    n = ts // C_PHASES
    base = C_HALO - (C_CONV - 1)

    r = r_ref[...]
    h = _rms(r, nrm_ref[...]).astype(_bf16)

    def glu(q):
        z_v, z_g = [_dot(h, win_ref[:, c0:c0 + C_GROUP]) + bin_ref[:, c0:c0 + C_GROUP]
                    for c0 in (q * C_GROUP, C_WIDTH + q * C_GROUP)]
        y = z_v * jax.nn.sigmoid(z_g)
        for j in range(C_GROUP // LANES):
            s = q * (C_GROUP // LANES) + j
            ybuf_ref[s, C_HALO:C_HALO + ts, :] = y[:, j * LANES:(j + 1) * LANES]

    def conv(s):
        slab = ybuf_ref.at[s]
        lanes = slice(s * LANES, (s + 1) * LANES)
        taps = [jnp.broadcast_to(dww_ref[k:k + 1, lanes], (C_ROWS, LANES)) for k in range(C_CONV)]
        bias = jnp.broadcast_to(dwb_ref[:, lanes], (C_ROWS, LANES))
        for mb in range(n // C_ROWS):
            accs = [bias] * C_PHASES
            for i in range(C_PHASES + C_CONV - 1):
                yi = slab[pl.ds(base + i + C_PHASES * C_ROWS * mb, C_ROWS, stride=C_PHASES), :]
                for ph in range(C_PHASES):
                    k = i - ph
                    if 0 <= k < C_CONV:
                        accs[ph] = accs[ph] + yi * taps[k]
            for ph in range(C_PHASES):
                conv_ref[ph * n + mb * C_ROWS:ph * n + (mb + 1) * C_ROWS, lanes] = accs[ph]
        slab[0:C_HALO, :] = slab[ts:ts + C_HALO, :]

    glu(0)
    for q in range(n_groups):
        if q + 1 < n_groups:
            glu(q + 1)
        for j in range(C_GROUP // LANES):
            conv(q * (C_GROUP // LANES) + j)

    y = _layernorm(conv_ref[...], lng_ref[...], lnb_ref[...])
    y = _natural_rows(y * jax.nn.sigmoid(y), unperm_ref, C_PHASES).astype(_bf16)
    o_ref[...] = r + _dot(y, wout_ref[...]) + bout_ref[...]


def _odd_kernel(r_ref, *rest):
    params, o_ref, (ybuf_ref, conv_ref, unperm_ref) = rest[:-4], rest[-4], rest[-3:]

    @pl.when(pl.program_id(1) == 0)
    def _():
        ybuf_ref[:, 0:C_HALO, :] = jnp.zeros((C_WIDTH // LANES, C_HALO, LANES), _f32)

    _for_each_tile(r_ref.shape[0], lambda rows: _odd_tile(
        r_ref.at[rows], *params, o_ref.at[rows], ybuf_ref, conv_ref, unperm_ref))


def _ffn_tile(r_ref, p_ref, nrm_ref, wup_ref, dww_ref, dwb_ref, wdn_ref, plen_ref, wg_ref, wp_ref,
              fin_ref, o_ref, carry_ref, buf_ref, unperm_ref, *, final):
    ts = r_ref.shape[0]

    pe = _dot(p_ref[...].astype(_bf16), wp_ref[...])
    r = r_ref[...]
    h = _rms(r, nrm_ref[...]).astype(_bf16)
    n_slabs = FF_CHUNK // LANES

    def up(c):
        for half in range(2):
            col0 = half * D_FF + c * FF_CHUNK
            z = _dot(h, wup_ref[:, col0:col0 + FF_CHUNK])
            for s in range(n_slabs):
                slab = buf_ref.at[c % FF_SLOTS, half, s]
                lo = col0 + s * LANES
                slab[0:SUBLANES, :] = carry_ref[:, lo:lo + LANES]
                slab[SUBLANES:SUBLANES + ts, :] = z[:, s * LANES:(s + 1) * LANES]

    def gated(c):
        cols = []
        for s in range(n_slabs):
            convs = []
            for half in range(2):
                lo = half * D_FF + c * FF_CHUNK + s * LANES
                slab = buf_ref.at[c % FF_SLOTS, half, s]
                carry_ref[:, lo:lo + LANES] = slab[ts:ts + SUBLANES, :]
                convs.append(_conv3_phased(slab, ts, dww_ref[:, lo:lo + LANES])
                             + dwb_ref[:, lo:lo + LANES])
            g, u = convs
            cols.append(g * jax.nn.sigmoid(g) * u)
        return jnp.concatenate(cols, axis=1).astype(_bf16)

    for c in range(FF_SLOTS - 1):
        up(c)
    acc = None
    for c in range(N_FF_CHUNKS):
        if c + FF_SLOTS - 1 < N_FF_CHUNKS:
            up(c + FF_SLOTS - 1)
        d = _dot(gated(c), wdn_ref[c * FF_CHUNK:(c + 1) * FF_CHUNK, :])
        acc = d if acc is None else acc + d
    r2 = r + _natural_rows(acc, unperm_ref)

    half = ts // 2
    for lo in (0, half):
        r2h = r2[lo:lo + half]
        gate = jax.nn.sigmoid(_dot(_rms(r2h, plen_ref[...]).astype(_bf16), wg_ref[...]))
        r3 = r2h + gate * pe[lo:lo + half]
        if final:
            r3 = _rms(r3, fin_ref[...])
        o_ref[lo:lo + half, :] = r3


def _ffn_kernel(r_ref, p_ref, *rest, final):
    params, o_ref, (carry_ref, buf_ref, unperm_ref) = rest[:-4], rest[-4], rest[-3:]

    @pl.when(pl.program_id(1) == 0)
    def _():
        carry_ref[...] = jnp.zeros_like(carry_ref)

    _for_each_tile(r_ref.shape[0], lambda rows: _ffn_tile(
        r_ref.at[rows], p_ref.at[rows], *params, o_ref.at[rows], carry_ref, buf_ref, unperm_ref,
        final=final))


N_EVEN_PARAMS = 8
N_FFN_PARAMS = 9


def _even_ffn_kernel(r_ref, p_ref, *rest):
    mixer_params = rest[:N_EVEN_PARAMS]
    ffn_params = rest[N_EVEN_PARAMS:N_EVEN_PARAMS + N_FFN_PARAMS]
    o_ref = rest[N_EVEN_PARAMS + N_FFN_PARAMS]
    mid_ref, mix_buf_ref, mix_unperm_ref, carry_ref, buf_ref, unperm_ref = (
        rest[N_EVEN_PARAMS + N_FFN_PARAMS + 1:])

    @pl.when(pl.program_id(1) == 0)
    def _():
        mix_buf_ref[:, 0:SUBLANES, :] = jnp.zeros((B_WIDTH // LANES, SUBLANES, LANES), _f32)
        carry_ref[...] = jnp.zeros_like(carry_ref)

    def tile(rows):
        _even_tile(r_ref.at[rows], *mixer_params, mid_ref, mix_buf_ref, mix_unperm_ref)
        _ffn_tile(mid_ref, p_ref.at[rows], *ffn_params, o_ref.at[rows], carry_ref, buf_ref,
                  unperm_ref, final=False)

    _for_each_tile(r_ref.shape[0], tile, tile_rows=MERGED_TILE)


def _rows(n, layer=None, block=SEQ_BLOCK):
    if layer is None:
        return pl.BlockSpec((None, block, n), lambda b, j: (b, j, 0))
    return pl.BlockSpec((None, None, block, n), lambda b, j: (layer, b, j, 0))


class _Layer(NamedTuple):
    stacked: jax.Array
    layer: int


def _whole(a):
    if isinstance(a, _Layer):
        zeros = (0,) * (a.stacked.ndim - 1)
        return pl.BlockSpec((None,) + a.stacked.shape[1:], lambda b, j: (a.layer,) + zeros,
                            pipeline_mode=pl.Buffered(1))
    nd = a.ndim
    return pl.BlockSpec(a.shape, lambda b, j: (0,) * nd, pipeline_mode=pl.Buffered(1))


def _call(body, name, batch, seq, row_inputs, params, scratch, row_layers=None, block=SEQ_BLOCK):
    grid = (batch, seq // block)
    row_layers = row_layers or [None] * len(row_inputs)
    return pl.pallas_call(
        body,
        name=name,
        grid=grid,
        in_specs=([_rows(a.shape[-1], layer, block) for a, layer in zip(row_inputs, row_layers)]
                  + [_whole(a) for a in params]),
        out_specs=_rows(D_MODEL, None, block),
        out_shape=jax.ShapeDtypeStruct((batch, seq, D_MODEL), _f32),
        scratch_shapes=scratch,
        compiler_params=pltpu.CompilerParams(
            dimension_semantics=("arbitrary", "arbitrary"),
            vmem_limit_bytes=VMEM_LIMIT_BYTES),
    )(*row_inputs, *[a.stacked if isinstance(a, _Layer) else a for a in params])


def _row(v):
    return v.reshape(1, -1).astype(_f32)


def _cast_kernel(*refs):
    n = len(refs) // 2
    for x_ref, o_ref in zip(refs[:n], refs[n:]):
        o_ref[...] = x_ref[...].astype(o_ref.dtype)


def _to_bf16(ws):
    def spec(w):
        layers, rows, cols = w.shape
        return pl.BlockSpec((layers, rows // CAST_STEPS, cols), lambda i: (0, i, 0))

    return pl.pallas_call(
        _cast_kernel,
        name="to_bf16",
        grid=(CAST_STEPS,),
        in_specs=[spec(w) for w in ws],
        out_specs=[spec(w) for w in ws],
        out_shape=[jax.ShapeDtypeStruct(w.shape, _bf16) for w in ws],
        compiler_params=pltpu.CompilerParams(vmem_limit_bytes=CAST_VMEM_LIMIT_BYTES),
    )(*ws)


def _even_layer(r, norm, w_in, a_ws, a_bs, a_ln_g, a_ln_b, b_conv_w, w_out):
    batch, seq, _ = r.shape
    params = [_row(norm), w_in, a_ws, a_bs.T, _row(a_ln_g), _row(a_ln_b), b_conv_w, w_out]
    scratch = [pltpu.VMEM((B_WIDTH // LANES, SUBLANES + SEQ_TILE, LANES), _f32),
               pltpu.VMEM((B_WIDTH // LANES, SEQ_TILE, LANES), _f32)]
    return _call(_even_kernel, "even_mixer", batch, seq, [r], params, scratch)


def _even_ffn_layer(r, p, layer, norm, w_in, a_ws, a_bs, a_ln_g, a_ln_b, b_conv_w, w_out,
                    ffn_norm, w_up, dw_w, dw_b, w_down, ple_norm, w_g, w_p, final_norm):
    batch, seq, _ = r.shape
    params = [_row(norm), w_in, a_ws, a_bs.T, _row(a_ln_g), _row(a_ln_b), b_conv_w, w_out,
              _row(ffn_norm), w_up, dw_w, _row(dw_b), w_down, _row(ple_norm), w_g, w_p,
              _row(final_norm)]
    t = MERGED_TILE
    scratch = [pltpu.VMEM((t, D_MODEL), _f32),
               pltpu.VMEM((B_WIDTH // LANES, SUBLANES + t, LANES), _f32),
               pltpu.VMEM((B_WIDTH // LANES, t, LANES), _f32),
               pltpu.VMEM((SUBLANES, 2 * D_FF), _f32),
               pltpu.VMEM((FF_SLOTS, 2, FF_CHUNK // LANES, SUBLANES + t, LANES), _f32),
               pltpu.VMEM((D_MODEL // LANES, t, LANES), _f32)]
    return _call(_even_ffn_kernel, "even_layer", batch, seq, [r, p], params, scratch,
                 row_layers=[None, layer], block=MERGED_BLOCK)


def _odd_layer(r, norm, w_in, b_in, dw_w, dw_b, ln_g, ln_b, w_out, b_out):
    batch, seq, _ = r.shape
    params = [_row(norm), w_in, _row(b_in), dw_w, _row(dw_b), _row(ln_g), _row(ln_b), w_out,
              _row(b_out)]
    scratch = [pltpu.VMEM((C_WIDTH // LANES, C_HALO + SEQ_TILE, LANES), _f32),
               pltpu.VMEM((SEQ_TILE, C_WIDTH), _f32),
               pltpu.VMEM((C_WIDTH // LANES, SEQ_TILE, LANES), _f32)]
    return _call(_odd_kernel, "odd_mixer", batch, seq, [r], params, scratch)


def _ffn_layer(r, p, layer, norm, w_up, dw_w, dw_b, w_down, ple_norm, w_g, w_p, final_norm, final):
    batch, seq, _ = r.shape
    params = [_row(norm), w_up, dw_w, _row(dw_b), w_down, _row(ple_norm), w_g, w_p,
              _row(final_norm)]
    scratch = [pltpu.VMEM((SUBLANES, 2 * D_FF), _f32),
               pltpu.VMEM((FF_SLOTS, 2, FF_CHUNK // LANES, SUBLANES + SEQ_TILE, LANES), _f32),
               pltpu.VMEM((D_MODEL // LANES, SEQ_TILE, LANES), _f32)]
    return _call(functools.partial(_ffn_kernel, final=final), "ffn_final" if final else "ffn",
                 batch, seq, [r, p], params, scratch, row_layers=[None, layer])


def kernel(x, p, ev_norm, ev_w_in, ev_a_ws, ev_a_bs, ev_a_ln_g, ev_a_ln_b, ev_b_conv_w, ev_w_out,
           od_norm, od_w_in, od_b_in, od_dw_w, od_dw_b, od_ln_g, od_ln_b, od_w_out, od_b_out,
           ffn_norm, ffn_w_up, ffn_dw_w, ffn_dw_b, ffn_w_down, ple_w_p, ple_norm, ple_w_g,
           final_norm):
    depth = p.shape[0]
    ev_w_in, ev_w_out, od_w_in, od_w_out, ffn_w_up, ffn_w_down, ple_w_g, ple_w_p = _to_bf16(
        [ev_w_in, ev_w_out, od_w_in, od_w_out, ffn_w_up, ffn_w_down, ple_w_g, ple_w_p])
    r = x
    for i in range(depth):
        j = i // 2
        if i % 2 == 0 and i < depth - 1:
            r = _even_ffn_layer(r, p, i, ev_norm[j], _Layer(ev_w_in, j), ev_a_ws[j], ev_a_bs[j],
                                ev_a_ln_g[j], ev_a_ln_b[j], ev_b_conv_w[j], _Layer(ev_w_out, j),
                                ffn_norm[i], _Layer(ffn_w_up, i), ffn_dw_w[i], ffn_dw_b[i],
                                _Layer(ffn_w_down, i), ple_norm[i], _Layer(ple_w_g, i),
                                _Layer(ple_w_p, i), final_norm)
            continue
        if i % 2 == 0:
            r = _even_layer(r, ev_norm[j], _Layer(ev_w_in, j), ev_a_ws[j], ev_a_bs[j],
                            ev_a_ln_g[j], ev_a_ln_b[j], ev_b_conv_w[j], _Layer(ev_w_out, j))
        else:
            r = _odd_layer(r, od_norm[j], _Layer(od_w_in, j), od_b_in[j], od_dw_w[j], od_dw_b[j],
                           od_ln_g[j], od_ln_b[j], _Layer(od_w_out, j), od_b_out[j])
        r = _ffn_layer(r, p, i, ffn_norm[i], _Layer(ffn_w_up, i), ffn_dw_w[i], ffn_dw_b[i],
                       _Layer(ffn_w_down, i), ple_norm[i], _Layer(ple_w_g, i),
                       _Layer(ple_w_p, i), final_norm, final=(i == depth - 1))
    return r
```
